```python
import math
import jax, jax.numpy as jnp
from jax import lax
import numpy as np

D_MODEL = 1024
BATCH = 16
SEQ = 2048
DEPTH = 4

GRID_W = 64
CTX_LEN = 256

N_HEADS_NA = 8
HEAD_DIM_NA = 64
D_NA = N_HEADS_NA * HEAD_DIM_NA
WIN_R = 8
WIN_C = 16
Q_BLK_C = 16
HALO_C = Q_BLK_C + WIN_C

N_HEADS_SSD = 8
HEAD_DIM_SSD = 64
D_SSD = N_HEADS_SSD * HEAD_DIM_SSD
N_GROUPS_SSD = 2
D_STATE = 128
D_CONV = 5
CHUNK = 128
DT_MIN = 1e-3
DT_MAX = 1e-1
ROPE_BASE = 10000.0

D_MIX = D_NA + D_SSD
GN = N_GROUPS_SSD * D_STATE
CONV_CH = D_SSD + 2 * GN
OFF_K = D_NA
OFF_V = 2 * D_NA
OFF_Z = 3 * D_NA
OFF_XBC = OFF_Z + D_SSD
OFF_DT = OFF_XBC + CONV_CH
D_IN_PROJ = OFF_DT + 2 * N_HEADS_SSD

D_FF = 2816
N_EXPERTS = 8
TOP_K = 2
D_FF_EXPERT = 3584
N_DENSE = (DEPTH + 1) // 2
N_MOE = DEPTH // 2

DEEPNORM_ALPHA = (2.0 * DEPTH) ** 0.25
DEEPNORM_BETA = (8.0 * DEPTH) ** -0.25
LN_EPS = 1e-5
RMS_EPS = 1e-5

kernel_name = 'hybrid_natten_ssd_moe_deepnorm_dit'


def _layernorm(x, g, b):
    xf = x.astype(jnp.float32)
    mu = jnp.mean(xf, axis=-1, keepdims=True)
    var = jnp.mean(jnp.square(xf - mu), axis=-1, keepdims=True)
    y = (xf - mu) * lax.rsqrt(var + LN_EPS) * g.astype(jnp.float32) + b.astype(jnp.float32)
    return y.astype(x.dtype)


def _split_proj(p):
    return (p[..., :OFF_K], p[..., OFF_K:OFF_V], p[..., OFF_V:OFF_Z],
            p[..., OFF_Z:OFF_XBC], p[..., OFF_XBC:OFF_DT], p[..., OFF_DT:])


def _heads(t):
    b, l, _ = t.shape
    return t.reshape(b, l, N_HEADS_NA, HEAD_DIM_NA)


def _context_attention(q, k, v):
    b, l = q.shape[0], q.shape[1]
    s = jnp.einsum('bqhd,bkhd->bhqk', q, k).astype(jnp.float32) * (HEAD_DIM_NA ** -0.5)
    p = jax.nn.softmax(s, axis=-1).astype(v.dtype)
    return jnp.einsum('bhqk,bkhd->bqhd', p, v).reshape(b, l, D_NA)


def _neighbourhood_attention(q, k, v, k_ctx, v_ctx, rpb, rows):
    b = q.shape[0]
    kr = min(WIN_R, rows)
    ncb = GRID_W // Q_BLK_C
    scale = HEAD_DIM_NA ** -0.5
    cols = np.arange(GRID_W)
    col_start = np.clip(cols - WIN_C // 2, 0, GRID_W - WIN_C).reshape(ncb, Q_BLK_C)
    halo_start = np.clip(np.arange(ncb) * Q_BLK_C - WIN_C // 2, 0, GRID_W - HALO_C)
    halo_cols = halo_start[:, None] + np.arange(HALO_C)
    key_col = halo_cols[:, None, :]
    in_win = jnp.asarray((key_col >= col_start[:, :, None]) & (key_col < col_start[:, :, None] + WIN_C))
    col_bias_idx = np.clip(key_col - cols.reshape(ncb, Q_BLK_C)[:, :, None] + WIN_C - 1, 0, 2 * WIN_C - 2)
    qg = q.reshape(b, rows, ncb, Q_BLK_C, N_HEADS_NA, HEAD_DIM_NA)
    kg = k.reshape(b, rows, GRID_W, N_HEADS_NA, HEAD_DIM_NA)
    vg = v.reshape(b, rows, GRID_W, N_HEADS_NA, HEAD_DIM_NA)
    n_loc = kr * HALO_C

    def row_block(i):
        r0 = jnp.clip(i - kr // 2, 0, rows - kr)
        k_rows = lax.dynamic_slice_in_dim(kg, r0, kr, axis=1)
        v_rows = lax.dynamic_slice_in_dim(vg, r0, kr, axis=1)
        k_halo = k_rows[:, :, halo_cols]
        v_halo = v_rows[:, :, halo_cols]
        q_row = lax.dynamic_index_in_dim(qg, i, axis=1, keepdims=False)
        s_loc = jnp.einsum('bcqhd,brckhd->bhcqrk', q_row, k_halo).astype(jnp.float32) * scale
        row_bias_idx = r0 + jnp.arange(kr) - i + WIN_R - 1
        bias = rpb[:, row_bias_idx][:, :, col_bias_idx].transpose(0, 2, 3, 1, 4)
        s_loc = jnp.where(in_win[:, :, None, :], s_loc + bias.astype(jnp.float32), -jnp.inf)
        s_ctx = jnp.einsum('bcqhd,bmhd->bhcqm', q_row, k_ctx).astype(jnp.float32) * scale
        s = jnp.concatenate([s_loc.reshape(s_loc.shape[:4] + (n_loc,)), s_ctx], axis=-1)
        p = jax.nn.softmax(s, axis=-1).astype(v.dtype)
        p_loc = p[..., :n_loc].reshape(s_loc.shape)
        o = (jnp.einsum('bhcqrk,brckhd->bcqhd', p_loc, v_halo)
             + jnp.einsum('bhcqm,bmhd->bcqhd', p[..., n_loc:], v_ctx))
        return o.reshape(b, GRID_W, N_HEADS_NA, HEAD_DIM_NA)

    out = lax.map(row_block, jnp.arange(rows))
    return out.transpose(1, 0, 2, 3, 4).reshape(b, rows * GRID_W, D_NA)


def _axial_rope(u):
    l = u.shape[1]
    t = jnp.arange(l)
    half = D_STATE // 2
    quarter = half // 2
    inv_freq = ROPE_BASE ** (-jnp.arange(quarter, dtype=jnp.float32) / quarter)
    ang_r = (t // GRID_W).astype(jnp.float32)[:, None] * inv_freq
    ang_c = (t % GRID_W).astype(jnp.float32)[:, None] * inv_freq

    def rot(w, ang):
        cos = jnp.cos(ang)[:, None, :].astype(w.dtype)
        sin = jnp.sin(ang)[:, None, :].astype(w.dtype)
        w1, w2 = w[..., :quarter], w[..., quarter:]
        return jnp.concatenate([w1 * cos - w2 * sin, w1 * sin + w2 * cos], axis=-1)

    return jnp.concatenate([rot(u[..., :half], ang_r), rot(u[..., half:], ang_c)], axis=-1)


def _dwconv(u, w, bias):
    k = w.shape[0]
    y = lax.conv_general_dilated(u, w[:, None, :].astype(u.dtype), window_strides=(1,),
                                 padding=[(k // 2, k // 2)], dimension_numbers=('NWC', 'WIO', 'NWC'),
                                 feature_group_count=u.shape[-1])
    return y + bias


def _segsum(a):
    t = a.shape[-1]
    cs = jnp.cumsum(a, axis=-1)
    diff = cs[..., :, None] - cs[..., None, :]
    return jnp.where(jnp.tril(jnp.ones((t, t), dtype=bool)), diff, -jnp.inf)


def _ssd_chunked(xh, da, bm, cm, h0):
    b, l, nh, p = xh.shape
    g, n = bm.shape[2], bm.shape[3]
    r = nh // g
    nc = l // CHUNK
    dtp = xh.dtype
    xc = xh.reshape(b, nc, CHUNK, g, r, p)
    bc = bm.reshape(b, nc, CHUNK, g, n)
    cc = cm.reshape(b, nc, CHUNK, g, n)
    a = da.reshape(b, nc, CHUNK, g, r).transpose(0, 3, 4, 1, 2)
    a_cs = jnp.cumsum(a, axis=-1)
    decay_in = jnp.exp(_segsum(a)).astype(dtp)
    cb = jnp.einsum('bclgn,bcsgn->bgcls', cc, bc)
    y_diag = jnp.einsum('bgrcls,bcsgrp->bclgrp', cb[:, :, None] * decay_in, xc)
    to_end = jnp.exp(a_cs[..., -1:] - a_cs).astype(dtp).transpose(0, 3, 4, 1, 2)[..., None]
    states = jnp.einsum('bclgn,bclgrp->bcgrpn', bc, xc * to_end)
    states = jnp.concatenate([h0.reshape(b, 1, g, r, p, n).astype(dtp), states], axis=1)
    totals = jnp.pad(a_cs[..., -1], ((0, 0), (0, 0), (0, 0), (1, 0)))
    decay_ch = jnp.exp(_segsum(totals)).astype(dtp)
    carried = jnp.einsum('bgrzc,bcgrpn->bzgrpn', decay_ch, states)
    from_start = jnp.exp(a_cs).astype(dtp).transpose(0, 3, 4, 1, 2)[..., None]
    y_off = jnp.einsum('bclgn,bcgrpn->bclgrp', cc, carried[:, :-1]) * from_start
    y = (y_diag + y_off).reshape(b, l, nh, p)
    return y, carried[:, -1].reshape(b, nh, p, n)


def _ssd_stream(z, xbc, dt_raw, conv_w, conv_b, dt_bias, a_log, d_skip, norm_w, h0_f, h0_b, rotary):
    b, l, _ = xbc.shape
    u = jax.nn.silu(_dwconv(xbc, conv_w, conv_b))
    xs = u[..., :D_SSD].reshape(b, l, N_HEADS_SSD, HEAD_DIM_SSD)
    bm = u[..., D_SSD:D_SSD + GN].reshape(b, l, N_GROUPS_SSD, D_STATE)
    cm = u[..., D_SSD + GN:].reshape(b, l, N_GROUPS_SSD, D_STATE)
    if rotary:
        bm = _axial_rope(bm)
        cm = _axial_rope(cm)
    dt = jax.nn.softplus(dt_raw.astype(jnp.float32).reshape(b, l, 2, N_HEADS_SSD) + dt_bias.astype(jnp.float32))
    da = dt * (-jnp.exp(a_log.astype(jnp.float32)))
    dtc = dt.astype(xs.dtype)
    y_f, h_f = _ssd_chunked(xs * dtc[:, :, 0, :, None], da[:, :, 0], bm, cm, h0_f)
    fl = lambda t: jnp.flip(t, axis=1)
    y_b, h_b = _ssd_chunked(fl(xs * dtc[:, :, 1, :, None]), fl(da[:, :, 1]), fl(bm), fl(cm), h0_b)
    y = y_f + fl(y_b) + d_skip[:, None] * xs
    yg = (y.reshape(b, l, D_SSD) * jax.nn.silu(z)).astype(jnp.float32)
    yg = yg.reshape(b, l, N_GROUPS_SSD, D_SSD // N_GROUPS_SSD)
    yg = yg * lax.rsqrt(jnp.mean(yg * yg, axis=-1, keepdims=True) + RMS_EPS)
    out = (yg.reshape(b, l, D_SSD) * norm_w.astype(jnp.float32)).astype(z.dtype)
    return out, h_f, h_b


def _swiglu(h, wg, wu, wd):
    return (jax.nn.silu(h @ wg) * (h @ wu)) @ wd


def _moe(h, router, wg, wu, wd):
    logits = (h @ router).astype(jnp.float32)
    top_v, top_i = lax.top_k(logits, TOP_K)
    gates = jax.nn.softmax(top_v, axis=-1)
    combine = jnp.sum(jax.nn.one_hot(top_i, N_EXPERTS, dtype=jnp.float32) * gates[..., None], axis=-2)
    combine = combine.astype(h.dtype)
    out = jnp.zeros_like(h)
    for e in range(N_EXPERTS):
        out = out + combine[..., e:e + 1] * _swiglu(h, wg[e], wu[e], wd[e])
    return out


def setup_inputs(seed: int = 0) -> dict:
    key = jax.random.key(seed)
    ks = jax.random.split(key, 24)
    f32 = jnp.float32
    nrm = lambda k, shape, s: jax.random.normal(k, shape, f32) * s
    dt0 = jnp.exp(jax.random.uniform(ks[11], (DEPTH, 2, N_HEADS_SSD), f32, math.log(DT_MIN), math.log(DT_MAX)))
    return {
        'x': nrm(ks[0], (BATCH, SEQ, D_MODEL), 1.0),
        'c': nrm(ks[1], (BATCH, D_MODEL), 1.0),
        'ctx': nrm(ks[2], (BATCH, CTX_LEN, D_MODEL), 1.0),
        'c_ctx': nrm(ks[3], (D_MODEL,), 1.0),
        'w_mod': nrm(ks[4], (DEPTH, D_MODEL, 6 * D_MODEL), 0.5 * D_MODEL ** -0.5),
        'b_mod': nrm(ks[5], (DEPTH, 6 * D_MODEL), 0.01),
        'w_in': nrm(ks[6], (DEPTH, D_MODEL, D_IN_PROJ), D_MODEL ** -0.5),
        'w_out': nrm(ks[7], (DEPTH, D_MIX, D_MODEL), DEEPNORM_BETA * D_MIX ** -0.5),
        'na_rpb': nrm(ks[8], (DEPTH, N_HEADS_NA, 2 * WIN_R - 1, 2 * WIN_C - 1), 0.1),
        'conv_w': nrm(ks[9], (DEPTH, D_CONV, CONV_CH), D_CONV ** -0.5),
        'conv_b': nrm(ks[10], (DEPTH, CONV_CH), 0.01),
        'dt_bias': dt0 + jnp.log(-jnp.expm1(-dt0)),
        'a_log': jnp.log(jax.random.uniform(ks[12], (DEPTH, 2, N_HEADS_SSD), f32, 1.0, 16.0)),
        'd_skip': 1.0 + nrm(ks[13], (DEPTH, N_HEADS_SSD), 0.01),
        'ssd_norm_w': 1.0 + nrm(ks[14], (DEPTH, D_SSD), 0.01),
        'ln_g': 1.0 + nrm(ks[15], (DEPTH, 2, D_MODEL), 0.01),
        'ln_b': nrm(ks[16], (DEPTH, 2, D_MODEL), 0.01),
        'ffn_w_gate': nrm(ks[17], (N_DENSE, D_MODEL, D_FF), D_MODEL ** -0.5),
        'ffn_w_up': nrm(ks[18], (N_DENSE, D_MODEL, D_FF), D_MODEL ** -0.5),
        'ffn_w_down': nrm(ks[19], (N_DENSE, D_FF, D_MODEL), DEEPNORM_BETA * D_FF ** -0.5),
        'router_w': nrm(ks[20], (N_MOE, D_MODEL, N_EXPERTS), D_MODEL ** -0.5),
        'moe_w_gate': nrm(ks[21], (N_MOE, N_EXPERTS, D_MODEL, D_FF_EXPERT), D_MODEL ** -0.5),
        'moe_w_up': nrm(ks[22], (N_MOE, N_EXPERTS, D_MODEL, D_FF_EXPERT), D_MODEL ** -0.5),
        'moe_w_down': nrm(ks[23], (N_MOE, N_EXPERTS, D_FF_EXPERT, D_MODEL), DEEPNORM_BETA * D_FF_EXPERT ** -0.5),
    }


def reference(x, c, ctx, c_ctx, w_mod, b_mod, w_in, w_out, na_rpb, conv_w, conv_b, dt_bias, a_log,
              d_skip, ssd_norm_w, ln_g, ln_b, ffn_w_gate, ffn_w_up, ffn_w_down, router_w,
              moe_w_gate, moe_w_up, moe_w_down):
    b, seq, _ = x.shape
    rows = seq // GRID_W
    xc = ctx
    for layer in range(DEPTH):
        last = layer == DEPTH - 1
        mod = (jax.nn.silu(c) @ w_mod[layer] + b_mod[layer])[:, None, :]
        mod_c = (jax.nn.silu(c_ctx) @ w_mod[layer] + b_mod[layer])[None, None, :]
        sh1, sc1, g1, sh2, sc2, g2 = jnp.split(mod, 6, axis=-1)
        sh1c, sc1c, g1c, sh2c, sc2c, g2c = jnp.split(mod_c, 6, axis=-1)
        q, k, v, z, xbc, dtr = _split_proj((x * (1.0 + sc1) + sh1) @ w_in[layer])
        qc, kc, vc, zc, xbcc, dtrc = _split_proj((xc * (1.0 + sc1c) + sh1c) @ w_in[layer])
        kc_h, vc_h = _heads(kc), _heads(vc)
        a_lat = _neighbourhood_attention(_heads(q), _heads(k), _heads(v), kc_h, vc_h, na_rpb[layer], rows)
        h_zero = jnp.zeros((b, N_HEADS_SSD, HEAD_DIM_SSD, D_STATE), x.dtype)
        s_ctx, hf, hb = _ssd_stream(zc, xbcc, dtrc, conv_w[layer], conv_b[layer], dt_bias[layer], a_log[layer],
                                    d_skip[layer], ssd_norm_w[layer], h_zero, h_zero, False)
        s_lat, _, _ = _ssd_stream(z, xbc, dtr, conv_w[layer], conv_b[layer], dt_bias[layer], a_log[layer],
                                  d_skip[layer], ssd_norm_w[layer], hf, hb, True)
        mix = jnp.concatenate([a_lat, s_lat], axis=-1) @ w_out[layer]
        x = _layernorm(DEEPNORM_ALPHA * x + (1.0 + g1) * mix, ln_g[layer, 0], ln_b[layer, 0])
        h2 = x * (1.0 + sc2) + sh2
        if layer % 2 == 0:
            wg, wu, wd = ffn_w_gate[layer // 2], ffn_w_up[layer // 2], ffn_w_down[layer // 2]
            chan = lambda t: _swiglu(t, wg, wu, wd)
        else:
            rw, eg, eu, ed = router_w[layer // 2], moe_w_gate[layer // 2], moe_w_up[layer // 2], moe_w_down[layer // 2]
            chan = lambda t: _moe(t, rw, eg, eu, ed)
        x = _layernorm(DEEPNORM_ALPHA * x + (1.0 + g2) * chan(h2), ln_g[layer, 1], ln_b[layer, 1])
        if not last:
            a_c = _context_attention(_heads(qc), kc_h, vc_h)
            mix_c = jnp.concatenate([a_c, s_ctx], axis=-1) @ w_out[layer]
            xc = _layernorm(DEEPNORM_ALPHA * xc + (1.0 + g1c) * mix_c, ln_g[layer, 0], ln_b[layer, 0])
            xc = _layernorm(DEEPNORM_ALPHA * xc + (1.0 + g2c) * chan(xc * (1.0 + sc2c) + sh2c),
                            ln_g[layer, 1], ln_b[layer, 1])
    return x
```

```python
import functools
import math

import numpy as np
import jax
import jax.numpy as jnp
from jax import lax
from jax.experimental import pallas as pl
from jax.experimental.pallas import tpu as pltpu

F32 = jnp.float32
BF16 = jnp.bfloat16
HIGHEST = lax.Precision.HIGHEST

D_MODEL = 1024
DEPTH = 4
GRID_W = 64
CTX_LEN = 256
N_HEADS_NA = 8
HEAD_DIM_NA = 64
D_NA = N_HEADS_NA * HEAD_DIM_NA
WIN_R = 8
WIN_C = 16
N_HEADS_SSD = 8
HEAD_DIM_SSD = 64
D_SSD = N_HEADS_SSD * HEAD_DIM_SSD
N_GROUPS_SSD = 2
D_STATE = 128
D_CONV = 5
CHUNK = 128
ROPE_BASE = 10000.0
GN = N_GROUPS_SSD * D_STATE
CONV_CH = D_SSD + 2 * GN
D_IN_PROJ = 3 * D_NA + D_SSD + CONV_CH + 2 * N_HEADS_SSD
N_EXPERTS = 8
TOP_K = 2
DEEPNORM_ALPHA = (2.0 * DEPTH) ** 0.25
LN_EPS = 1e-5
RMS_EPS = 1e-5

LANES = 128
SUBLANES = 8
VMEM_LIMIT = 56 * 1024 * 1024
DT_PAD = LANES
D_IN_PAD = D_IN_PROJ - 2 * N_HEADS_SSD + DT_PAD
OFF_Z = 3 * D_NA
OFF_XBC = OFF_Z + D_SSD
OFF_DT = OFF_XBC + CONV_CH
MOD_ROWS = 24
HEAD_PAIR = 2 * HEAD_DIM_NA


def _cparams(*sem):
    return pltpu.CompilerParams(dimension_semantics=sem, vmem_limit_bytes=VMEM_LIMIT)


def _silu(v):
    return v * jax.nn.sigmoid(v)


def _layernorm_rows(v, g, b):
    mu = jnp.mean(v, axis=-1, keepdims=True)
    d = v - mu
    var = jnp.mean(d * d, axis=-1, keepdims=True)
    return d * lax.rsqrt(var + LN_EPS) * g + b


def _row_is_ctx(tm, j):
    row = lax.broadcasted_iota(jnp.int32, (tm, 1), 0) + j * tm
    return row < CTX_LEN


def _mod_pick(is_ctx, modc_ref, mod_ref, seg):
    lo, hi = seg * D_MODEL, (seg + 1) * D_MODEL
    return jnp.where(is_ctx, modc_ref[0, :, lo:hi], mod_ref[0, :, lo:hi])


def _mod_kernel(c_ref, w_ref, b_ref, o_ref):
    cs = _silu(c_ref[...])
    o_ref[0] = jnp.dot(cs, w_ref[0], preferred_element_type=F32, precision=HIGHEST) + b_ref[0]


def _modulation(cvec, w_mod, b_mod):
    tn = D_MODEL
    return pl.pallas_call(
        _mod_kernel,
        grid=(DEPTH, 6 * D_MODEL // tn),
        in_specs=[
            pl.BlockSpec((MOD_ROWS, D_MODEL), lambda l, n: (0, 0)),
            pl.BlockSpec((1, D_MODEL, tn), lambda l, n: (l, 0, n)),
            pl.BlockSpec((1, 1, tn), lambda l, n: (l, 0, n)),
        ],
        out_specs=pl.BlockSpec((1, MOD_ROWS, tn), lambda l, n: (l, 0, n)),
        out_shape=jax.ShapeDtypeStruct((DEPTH, MOD_ROWS, 6 * D_MODEL), F32),
        compiler_params=_cparams("arbitrary", "arbitrary"),
        name="modulation",
    )(cvec, w_mod, b_mod.reshape(DEPTH, 1, 6 * D_MODEL))


def _in_proj_kernel(x_ref, mod_ref, modc_ref, w_ref, qkv_ref, z_ref, xbc_ref, dt_ref, *, tm):
    is_ctx = _row_is_ctx(tm, pl.program_id(1))
    sh = _mod_pick(is_ctx, modc_ref, mod_ref, 0)
    sc = _mod_pick(is_ctx, modc_ref, mod_ref, 1)
    h = (x_ref[0] * (1.0 + sc) + sh).astype(BF16)
    qkv_ref[0] = jnp.dot(h, w_ref[:, 0:OFF_Z], preferred_element_type=F32).astype(BF16)
    z_ref[0] = jnp.dot(h, w_ref[:, OFF_Z:OFF_XBC], preferred_element_type=F32)
    xbc_ref[0] = jnp.dot(h, w_ref[:, OFF_XBC:OFF_DT], preferred_element_type=F32)
    dt_ref[0] = jnp.dot(h, w_ref[:, OFF_DT:D_IN_PAD], preferred_element_type=F32)


def _in_proj(xa, mod, w_in_p, *, tm):
    b, rows, _ = xa.shape
    row_spec = lambda n: pl.BlockSpec((1, tm, n), lambda i, j: (i, j, 0))
    return pl.pallas_call(
        functools.partial(_in_proj_kernel, tm=tm),
        grid=(b, rows // tm),
        in_specs=[
            row_spec(D_MODEL),
            pl.BlockSpec((1, 1, 6 * D_MODEL), lambda i, j: (i, 0, 0)),
            pl.BlockSpec((1, 1, 6 * D_MODEL), lambda i, j: (MOD_ROWS - 1, 0, 0)),
            pl.BlockSpec((D_MODEL, D_IN_PAD), lambda i, j: (0, 0)),
        ],
        out_specs=[row_spec(OFF_Z), row_spec(D_SSD), row_spec(CONV_CH), row_spec(DT_PAD)],
        out_shape=[
            jax.ShapeDtypeStruct((b, rows, OFF_Z), BF16),
            jax.ShapeDtypeStruct((b, rows, D_SSD), F32),
            jax.ShapeDtypeStruct((b, rows, CONV_CH), F32),
            jax.ShapeDtypeStruct((b, rows, DT_PAD), F32),
        ],
        compiler_params=_cparams("arbitrary", "arbitrary"),
        name="in_proj",
    )(xa, mod, mod, w_in_p)


def _attn_head_pair(q_pair, keys_vals, biases):
    lane = lax.broadcasted_iota(jnp.int32, q_pair.shape, 1)
    outs = []
    for hh in range(2):
        in_head = (lane >= hh * HEAD_DIM_NA) & (lane < (hh + 1) * HEAD_DIM_NA)
        qm = jnp.where(in_head, q_pair, jnp.zeros_like(q_pair))
        scores = []
        for (k, _), bias in zip(keys_vals, biases[hh]):
            s = lax.dot_general(qm, k, (((1,), (1,)), ((), ())), preferred_element_type=F32)
            scores.append(s if bias is None else s + bias)
        m = scores[0].max(axis=-1, keepdims=True)
        for s in scores[1:]:
            m = jnp.maximum(m, s.max(axis=-1, keepdims=True))
        denom = jnp.zeros_like(m)
        acc = jnp.zeros(q_pair.shape, F32)
        for s, (_, v) in zip(scores, keys_vals):
            p = jnp.exp(s - m)
            denom = denom + p.sum(axis=-1, keepdims=True)
            acc = acc + jnp.dot(p.astype(BF16), v, preferred_element_type=F32)
        outs.append(acc / denom)
    return jnp.where(lane < HEAD_DIM_NA, outs[0], outs[1])


def _attn_kernel(qkv_ref, bias_ref, o_ref, *, rows_lat, with_ctx_out):
    kr = min(WIN_R, rows_lat)
    scale = HEAD_DIM_NA ** -0.5
    k_off, v_off = D_NA, 2 * D_NA
    n_pairs = N_HEADS_NA // 2

    def lat_row(i, carry):
        r0 = jnp.clip(i - kr // 2, 0, rows_lat - kr)
        d = i - r0
        q_base = pl.multiple_of(CTX_LEN + i * GRID_W, GRID_W)
        k_base = pl.multiple_of(CTX_LEN + r0 * GRID_W, GRID_W)
        for p in range(n_pairs):
            lo = p * HEAD_PAIR
            q = qkv_ref[0, pl.ds(q_base, GRID_W), lo:lo + HEAD_PAIR] * scale
            k_loc = qkv_ref[0, pl.ds(k_base, kr * GRID_W), k_off + lo:k_off + lo + HEAD_PAIR]
            v_loc = qkv_ref[0, pl.ds(k_base, kr * GRID_W), v_off + lo:v_off + lo + HEAD_PAIR]
            k_ctx = qkv_ref[0, 0:CTX_LEN, k_off + lo:k_off + lo + HEAD_PAIR]
            v_ctx = qkv_ref[0, 0:CTX_LEN, v_off + lo:v_off + lo + HEAD_PAIR]
            biases = [[bias_ref[d, 2 * p + hh], None] for hh in range(2)]
            o = _attn_head_pair(q, [(k_loc, v_loc), (k_ctx, v_ctx)], biases)
            o_ref[0, pl.ds(q_base, GRID_W), lo:lo + HEAD_PAIR] = o.astype(o_ref.dtype)
        return carry

    lax.fori_loop(0, rows_lat, lat_row, 0)

    if with_ctx_out:
        for p in range(n_pairs):
            lo = p * HEAD_PAIR
            q = qkv_ref[0, 0:CTX_LEN, lo:lo + HEAD_PAIR] * scale
            k_ctx = qkv_ref[0, 0:CTX_LEN, k_off + lo:k_off + lo + HEAD_PAIR]
            v_ctx = qkv_ref[0, 0:CTX_LEN, v_off + lo:v_off + lo + HEAD_PAIR]
            o = _attn_head_pair(q, [(k_ctx, v_ctx)], [[None], [None]])
            o_ref[0, 0:CTX_LEN, lo:lo + HEAD_PAIR] = o.astype(o_ref.dtype)
    else:
        o_ref[0, 0:CTX_LEN, :] = jnp.zeros((CTX_LEN, D_NA), o_ref.dtype)


def _attention(qkv, bias, *, with_ctx_out):
    b, rows, _ = qkv.shape
    rows_lat = (rows - CTX_LEN) // GRID_W
    return pl.pallas_call(
        functools.partial(_attn_kernel, rows_lat=rows_lat, with_ctx_out=with_ctx_out),
        grid=(b,),
        in_specs=[
            pl.BlockSpec((1, rows, OFF_Z), lambda i: (i, 0, 0)),
            pl.BlockSpec(bias.shape, lambda i: (0, 0, 0, 0)),
        ],
        out_specs=pl.BlockSpec((1, rows, D_NA), lambda i: (i, 0, 0)),
        out_shape=jax.ShapeDtypeStruct((b, rows, D_NA), BF16),
        compiler_params=_cparams("arbitrary"),
        name="attention",
    )(qkv, bias)


def _expand_rel_bias(rpb, rows_lat):
    kr = min(WIN_R, rows_lat)
    cols = np.arange(GRID_W)
    col_start = np.clip(cols - WIN_C // 2, 0, GRID_W - WIN_C)
    kc = cols[None, :]
    in_win = (kc >= col_start[:, None]) & (kc < col_start[:, None] + WIN_C)
    col_idx = np.clip(kc - cols[:, None] + WIN_C - 1, 0, 2 * WIN_C - 2)
    d = np.arange(WIN_R)[:, None]
    row_idx = np.clip(np.arange(kr)[None, :] - d + WIN_R - 1, 0, 2 * WIN_R - 2)
    t = rpb[:, row_idx]
    t = t[:, :, :, col_idx]
    t = jnp.where(jnp.asarray(in_win)[None, None, None], t.astype(F32), -jnp.inf)
    t = t.transpose(1, 0, 3, 2, 4)
    return t.reshape(WIN_R, N_HEADS_NA, GRID_W, kr * GRID_W)


def _ssd_kernel(xbc_ref, dt_ref, convw_ref, convb_ref, dtb_ref, alog_ref, dskip_ref, cos_ref, sin_ref,
                sel_ref, y_ref, htf_ref, htb_ref, stf_ref, c_ref, ef_ref, *, n_chunks):
    n_ctx_chunks = CTX_LEN // CHUNK
    hp = D_SSD // N_GROUPS_SSD
    htf_ref[...] = jnp.zeros_like(htf_ref)
    htb_ref[...] = jnp.zeros_like(htb_ref)

    li = lax.broadcasted_iota(jnp.int32, (CHUNK, CHUNK), 0)
    si = lax.broadcasted_iota(jnp.int32, (CHUNK, CHUNK), 1)
    lower = li >= si
    upper = li <= si
    ltri = lower.astype(F32)
    utri = upper.astype(F32)
    lane_c = lax.broadcasted_iota(jnp.int32, (CHUNK, LANES), 1)
    rope_lo = (lane_c % (D_STATE // 2)) < (D_STATE // 4)
    pair_lane = lax.broadcasted_iota(jnp.int32, (CHUNK, HEAD_PAIR), 1)

    def rope(w, cos, sin):
        partner = jnp.where(rope_lo, pltpu.roll(w, LANES - D_STATE // 4, 1), pltpu.roll(w, D_STATE // 4, 1))
        return w * cos + partner * sin

    def local_pass(k, carry):
        c = jnp.where(k < n_ctx_chunks, n_ctx_chunks - 1 - k, n_chunks + n_ctx_chunks - 1 - k)
        base = pl.multiple_of(c * CHUNK, CHUNK)
        is_start = (c == 0) | (c == n_ctx_chunks)
        is_end = (c == n_ctx_chunks - 1) | (c == n_chunks - 1)
        cur = xbc_ref[0, pl.ds(base, CHUNK), :]
        prev_base = pl.multiple_of(jnp.maximum(base - SUBLANES, 0), SUBLANES)
        next_base = pl.multiple_of(jnp.minimum(base + CHUNK, (n_chunks - 1) * CHUNK + CHUNK - SUBLANES), SUBLANES)
        prev8 = jnp.where(is_start, 0.0, xbc_ref[0, pl.ds(prev_base, SUBLANES), :])
        next8 = jnp.where(is_end, 0.0, xbc_ref[0, pl.ds(next_base, SUBLANES), :])
        ext = jnp.concatenate([prev8, cur, next8], axis=0)
        conv = convb_ref[...]
        for t in range(D_CONV):
            lo = SUBLANES - D_CONV // 2 + t
            conv = conv + ext[lo:lo + CHUNK] * convw_ref[t:t + 1, :]
        u = _silu(conv)
        xs = u[:, 0:D_SSD]
        cos = cos_ref[pl.ds(base, CHUNK), :]
        sin = sin_ref[pl.ds(base, CHUNK), :]
        bm = [rope(u[:, D_SSD + g * D_STATE:D_SSD + (g + 1) * D_STATE], cos, sin) for g in range(N_GROUPS_SSD)]
        cm = [rope(u[:, D_SSD + GN + g * D_STATE:D_SSD + GN + (g + 1) * D_STATE], cos, sin)
              for g in range(N_GROUPS_SSD)]

        dt = jax.nn.softplus(dt_ref[0, pl.ds(base, CHUNK), :] + dtb_ref[...])
        a = dt * (-jnp.exp(alog_ref[...]))
        pre = jnp.dot(ltri, a, preferred_element_type=F32, precision=HIGHEST)
        suf = jnp.dot(utri, a, preferred_element_type=F32, precision=HIGHEST)
        acs = jnp.where(lane_c < N_HEADS_SSD, pre, suf)
        acs_t = acs.T
        dt_t = dt.T
        acs_x = jnp.dot(acs, sel_ref[...], preferred_element_type=F32, precision=HIGHEST)
        dt_x = jnp.dot(dt, sel_ref[...], preferred_element_type=F32, precision=HIGHEST)
        arg_f, arg_b = acs_x[:, 0:D_SSD], acs_x[:, D_SSD:2 * D_SSD]
        e_f = jnp.exp(arg_f)
        e_b = jnp.exp(arg_b)
        x_f = (xs * dt_x[:, 0:D_SSD] * jnp.exp(arg_f[CHUNK - 1:CHUNK, :] - arg_f)).astype(BF16)
        x_b = (xs * dt_x[:, D_SSD:2 * D_SSD] * jnp.exp(arg_b[0:1, :] - arg_b)).astype(BF16)

        xs16 = xs.astype(BF16)
        y_parts = []
        for g in range(N_GROUPS_SSD):
            bt = bm[g].T.astype(BF16)
            c16 = cm[g].astype(BF16)
            gmat = jnp.dot(c16, bt, preferred_element_type=F32)
            gs = slice(g * hp, (g + 1) * hp)
            heads = range(g * (N_HEADS_SSD // N_GROUPS_SSD), (g + 1) * (N_HEADS_SSD // N_GROUPS_SSD))
            ms = []
            for h in heads:
                jf, jb = h, N_HEADS_SSD + h
                wf = jnp.where(lower, jnp.exp(acs[:, jf:jf + 1] - acs_t[jf:jf + 1, :]), 0.0) * dt_t[jf:jf + 1, :]
                wb = jnp.where(upper, jnp.exp(acs[:, jb:jb + 1] - acs_t[jb:jb + 1, :]), 0.0) * dt_t[jb:jb + 1, :]
                ms.append((gmat * (wf + wb)).astype(BF16))
            y_g = []
            for q in range(len(ms) // 2):
                lo = g * hp + q * HEAD_PAIR
                xp = xs16[:, lo:lo + HEAD_PAIR]
                rhs = jnp.concatenate([jnp.where(pair_lane < HEAD_DIM_SSD, xp, jnp.zeros_like(xp)),
                                       jnp.where(pair_lane >= HEAD_DIM_SSD, xp, jnp.zeros_like(xp))], axis=0)
                lhs = jnp.concatenate([ms[2 * q], ms[2 * q + 1]], axis=1)
                y_g.append(jnp.dot(lhs, rhs, preferred_element_type=F32))
            y_diag = jnp.concatenate(y_g, axis=1)
            y_off = jnp.dot(c16, htb_ref[g].astype(BF16), preferred_element_type=F32) * e_b[:, gs]
            y_parts.append(y_diag + y_off)
            htb_ref[g] = htb_ref[g] * e_b[0:1, gs] + jnp.dot(bt, x_b[:, gs], preferred_element_type=F32)
            stf_ref[c, g] = jnp.dot(bt, x_f[:, gs], preferred_element_type=F32)
            c_ref[c, g] = c16
        ef_ref[c] = e_f
        y_ref[0, pl.ds(base, CHUNK), :] = jnp.concatenate(y_parts, axis=1) + dskip_ref[...] * xs
        return carry

    lax.fori_loop(0, n_chunks, local_pass, 0)

    def forward_pass(c, carry):
        base = pl.multiple_of(c * CHUNK, CHUNK)
        e_f = ef_ref[c]
        y_parts = []
        for g in range(N_GROUPS_SSD):
            gs = slice(g * hp, (g + 1) * hp)
            y_parts.append(jnp.dot(c_ref[c, g], htf_ref[g].astype(BF16), preferred_element_type=F32) * e_f[:, gs])
            htf_ref[g] = htf_ref[g] * e_f[CHUNK - 1:CHUNK, gs] + stf_ref[c, g]
        y_ref[0, pl.ds(base, CHUNK), :] = y_ref[0, pl.ds(base, CHUNK), :] + jnp.concatenate(y_parts, axis=1)
        return carry

    lax.fori_loop(0, n_chunks, forward_pass, 0)


def _ssd_scan(xbc, dt, conv_w, conv_b, dt_bias, a_log, d_skip, cos_t, sin_t, sel):
    b, rows, _ = xbc.shape
    n_chunks = rows // CHUNK
    hp = D_SSD // N_GROUPS_SSD
    full = lambda a: pl.BlockSpec(a.shape, lambda i: (0,) * a.ndim)
    pad16 = lambda v: jnp.pad(v.reshape(1, -1), ((0, 0), (0, DT_PAD - v.size)))
    conv_w8 = jnp.pad(conv_w, ((0, SUBLANES - D_CONV), (0, 0)))
    params = [conv_w8, conv_b.reshape(1, -1), pad16(dt_bias), pad16(a_log),
              jnp.repeat(d_skip, HEAD_DIM_SSD).reshape(1, -1), cos_t, sin_t, sel]
    return pl.pallas_call(
        functools.partial(_ssd_kernel, n_chunks=n_chunks),
        grid=(b,),
        in_specs=[pl.BlockSpec((1, rows, CONV_CH), lambda i: (i, 0, 0)),
                  pl.BlockSpec((1, rows, DT_PAD), lambda i: (i, 0, 0))] + [full(p) for p in params],
        out_specs=pl.BlockSpec((1, rows, D_SSD), lambda i: (i, 0, 0)),
        out_shape=jax.ShapeDtypeStruct((b, rows, D_SSD), F32),
        scratch_shapes=[
            pltpu.VMEM((N_GROUPS_SSD, D_STATE, hp), F32),
            pltpu.VMEM((N_GROUPS_SSD, D_STATE, hp), F32),
            pltpu.VMEM((n_chunks, N_GROUPS_SSD, D_STATE, hp), F32),
            pltpu.VMEM((n_chunks, N_GROUPS_SSD, CHUNK, D_STATE), BF16),
            pltpu.VMEM((n_chunks, CHUNK, D_SSD), F32),
        ],
        compiler_params=_cparams("arbitrary"),
        name="ssd_scan",
    )(xbc, dt, *params)


def _rope_tables(seq):
    half, quarter = D_STATE // 2, D_STATE // 4
    t = jnp.arange(seq)
    inv_freq = ROPE_BASE ** (-jnp.arange(quarter, dtype=F32) / quarter)
    ang_r = (t // GRID_W).astype(F32)[:, None] * inv_freq
    ang_c = (t % GRID_W).astype(F32)[:, None] * inv_freq
    cos = jnp.concatenate([jnp.cos(ang_r)] * 2 + [jnp.cos(ang_c)] * 2, axis=-1)
    sin = jnp.concatenate([-jnp.sin(ang_r), jnp.sin(ang_r), -jnp.sin(ang_c), jnp.sin(ang_c)], axis=-1)
    cos = jnp.concatenate([jnp.ones((CTX_LEN, 2 * half), F32), cos], axis=0)
    sin = jnp.concatenate([jnp.zeros((CTX_LEN, 2 * half), F32), sin], axis=0)
    return cos, sin


def _head_select_matrix():
    sel = np.zeros((DT_PAD, 2 * D_SSD), np.float32)
    for j in range(2 * N_HEADS_SSD):
        sel[j, j * HEAD_DIM_SSD:(j + 1) * HEAD_DIM_SSD] = 1.0
    return jnp.asarray(sel)


def _out_proj_kernel(*refs, tm, with_router):
    if with_router:
        (a_ref, y_ref, z_ref, x_ref, mod_ref, modc_ref, w_ref, nw_ref, g_ref, b_ref, r_ref,
         x1_ref, h2_ref, ei_ref, gt_ref) = refs
    else:
        a_ref, y_ref, z_ref, x_ref, mod_ref, modc_ref, w_ref, nw_ref, g_ref, b_ref, x1_ref, h2_ref = refs
    is_ctx = _row_is_ctx(tm, pl.program_id(1))
    yg = y_ref[0] * _silu(z_ref[0])
    gw = D_SSD // N_GROUPS_SSD
    parts = []
    for g in range(N_GROUPS_SSD):
        part = yg[:, g * gw:(g + 1) * gw]
        parts.append(part * lax.rsqrt(jnp.mean(part * part, axis=-1, keepdims=True) + RMS_EPS))
    s = (jnp.concatenate(parts, axis=1) * nw_ref[...]).astype(BF16)
    mix = (jnp.dot(a_ref[0], w_ref[0:D_NA, :], preferred_element_type=F32)
           + jnp.dot(s, w_ref[D_NA:D_NA + D_SSD, :], preferred_element_type=F32))
    g1 = _mod_pick(is_ctx, modc_ref, mod_ref, 2)
    x1 = _layernorm_rows(DEEPNORM_ALPHA * x_ref[0] + (1.0 + g1) * mix, g_ref[...], b_ref[...])
    x1_ref[0] = x1
    sh2 = _mod_pick(is_ctx, modc_ref, mod_ref, 3)
    sc2 = _mod_pick(is_ctx, modc_ref, mod_ref, 4)
    h2 = x1 * (1.0 + sc2) + sh2
    h2_ref[0] = h2.astype(h2_ref.dtype)
    if with_router:
        logits = jnp.dot(h2, r_ref[...], preferred_element_type=F32, precision=HIGHEST)
        lane = lax.broadcasted_iota(jnp.int32, logits.shape, 1)
        logits = jnp.where(lane < N_EXPERTS, logits, -jnp.inf)
        m1 = logits.max(axis=-1, keepdims=True)
        i1 = jnp.where(logits == m1, lane, LANES).min(axis=-1, keepdims=True)
        rest = jnp.where(lane == i1, -jnp.inf, logits)
        m2 = rest.max(axis=-1, keepdims=True)
        i2 = jnp.where(rest == m2, lane, LANES).min(axis=-1, keepdims=True)
        e2 = jnp.exp(m2 - m1)
        gate1 = 1.0 / (1.0 + e2)
        gate2 = e2 / (1.0 + e2)
        ei_ref[0] = jnp.where(lane == 0, i1, jnp.where(lane == 1, i2, 0))
        gt_ref[0] = jnp.where(lane == 0, gate1, jnp.where(lane == 1, gate2, 0.0))


def _out_proj(a, y, z, xa, mod, w_out, norm_w, ln_g, ln_b, router_p, *, tm, h2_dtype):
    b, rows, _ = xa.shape
    with_router = router_p is not None
    row_spec = lambda n: pl.BlockSpec((1, tm, n), lambda i, j: (i, j, 0))
    vec = lambda n: pl.BlockSpec((1, n), lambda i, j: (0, 0))
    in_specs = [
        row_spec(D_NA), row_spec(D_SSD), row_spec(D_SSD), row_spec(D_MODEL),
        pl.BlockSpec((1, 1, 6 * D_MODEL), lambda i, j: (i, 0, 0)),
        pl.BlockSpec((1, 1, 6 * D_MODEL), lambda i, j: (MOD_ROWS - 1, 0, 0)),
        pl.BlockSpec((D_NA + D_SSD, D_MODEL), lambda i, j: (0, 0)),
        vec(D_SSD), vec(D_MODEL), vec(D_MODEL),
    ]
    args = [a, y, z, xa, mod, mod, w_out, norm_w.reshape(1, -1), ln_g.reshape(1, -1), ln_b.reshape(1, -1)]
    out_specs = [row_spec(D_MODEL), row_spec(D_MODEL)]
    out_shape = [jax.ShapeDtypeStruct((b, rows, D_MODEL), F32), jax.ShapeDtypeStruct((b, rows, D_MODEL), h2_dtype)]
    if with_router:
        in_specs.append(pl.BlockSpec((D_MODEL, LANES), lambda i, j: (0, 0)))
        args.append(router_p)
        out_specs += [row_spec(LANES), row_spec(LANES)]
        out_shape += [jax.ShapeDtypeStruct((b, rows, LANES), jnp.int32), jax.ShapeDtypeStruct((b, rows, LANES), F32)]
    return pl.pallas_call(
        functools.partial(_out_proj_kernel, tm=tm, with_router=with_router),
        grid=(b, rows // tm),
        in_specs=in_specs,
        out_specs=out_specs,
        out_shape=out_shape,
        compiler_params=_cparams("arbitrary", "arbitrary"),
        name="out_proj",
    )(*args)


def _ffn_kernel(h_ref, x_ref, mod_ref, modc_ref, wg_ref, wu_ref, wd_ref, g_ref, b_ref, o_ref, *, tm, tf):
    is_ctx = _row_is_ctx(tm, pl.program_id(1))
    h = h_ref[0]
    d_ff = wg_ref.shape[1]
    acc = jnp.zeros((tm, D_MODEL), F32)
    for f in range(d_ff // tf):
        fs = slice(f * tf, (f + 1) * tf)
        gate = jnp.dot(h, wg_ref[:, fs], preferred_element_type=F32)
        up = jnp.dot(h, wu_ref[:, fs], preferred_element_type=F32)
        acc = acc + jnp.dot((_silu(gate) * up).astype(BF16), wd_ref[fs, :], preferred_element_type=F32)
    g2 = _mod_pick(is_ctx, modc_ref, mod_ref, 5)
    o_ref[0] = _layernorm_rows(DEEPNORM_ALPHA * x_ref[0] + (1.0 + g2) * acc, g_ref[...], b_ref[...])


def _dense_ffn(h2, x1, mod, wg, wu, wd, ln_g, ln_b, *, tm, tf):
    b, rows, _ = x1.shape
    d_ff = wg.shape[1]
    row_spec = lambda n: pl.BlockSpec((1, tm, n), lambda i, j: (i, j, 0))
    vec = lambda n: pl.BlockSpec((1, n), lambda i, j: (0, 0))
    const = lambda s: pl.BlockSpec(s, lambda i, j: (0, 0), pipeline_mode=pl.Buffered(1))
    return pl.pallas_call(
        functools.partial(_ffn_kernel, tm=tm, tf=tf),
        grid=(b, rows // tm),
        in_specs=[
            row_spec(D_MODEL), row_spec(D_MODEL),
            pl.BlockSpec((1, 1, 6 * D_MODEL), lambda i, j: (i, 0, 0)),
            pl.BlockSpec((1, 1, 6 * D_MODEL), lambda i, j: (MOD_ROWS - 1, 0, 0)),
            const((D_MODEL, d_ff)), const((D_MODEL, d_ff)), const((d_ff, D_MODEL)),
            vec(D_MODEL), vec(D_MODEL),
        ],
        out_specs=row_spec(D_MODEL),
        out_shape=jax.ShapeDtypeStruct((b, rows, D_MODEL), F32),
        compiler_params=_cparams("arbitrary", "arbitrary"),
        name="dense_ffn",
    )(h2, x1, mod, mod, wg, wu, wd, ln_g.reshape(1, -1), ln_b.reshape(1, -1))


def _gather_kernel(idx_ref, src_ref, dst_ref, sem, *, rows_per_step):
    base = pl.program_id(0) * rows_per_step

    def row_copy(r):
        return pltpu.make_async_copy(src_ref.at[pl.ds(idx_ref[r], 1)], dst_ref.at[pl.ds(base + r, 1)], sem)

    def start(r, carry):
        row_copy(r).start()
        return carry

    def wait(r, carry):
        row_copy(r).wait()
        return carry

    lax.fori_loop(0, rows_per_step, start, 0)
    lax.fori_loop(0, rows_per_step, wait, 0)


def _gather_rows(src, idx, *, rows_per_step):
    n = idx.shape[0]
    return pl.pallas_call(
        functools.partial(_gather_kernel, rows_per_step=rows_per_step),
        grid=(n // rows_per_step,),
        in_specs=[
            pl.BlockSpec((rows_per_step,), lambda i: (i,), memory_space=pltpu.SMEM),
            pl.BlockSpec(memory_space=pl.ANY),
        ],
        out_specs=pl.BlockSpec(memory_space=pl.ANY),
        out_shape=jax.ShapeDtypeStruct((n, src.shape[1]), src.dtype),
        scratch_shapes=[pltpu.SemaphoreType.DMA(())],
        compiler_params=_cparams("arbitrary"),
        name="gather_rows",
    )(idx, src)


def _expert_kernel(te_ref, nu_ref, x_ref, wg_ref, wu_ref, wd_ref, y_ref, x16_ref):
    i, f = pl.program_id(0), pl.program_id(1)
    live = i < nu_ref[0]

    @pl.when(f == 0)
    def _():
        y_ref[...] = jnp.zeros_like(y_ref)
        x16_ref[...] = x_ref[...].astype(BF16)

    @pl.when(live)
    def _():
        h = x16_ref[...]
        gate = jnp.dot(h, wg_ref[0], preferred_element_type=F32)
        up = jnp.dot(h, wu_ref[0], preferred_element_type=F32)
        y_ref[...] += jnp.dot((_silu(gate) * up).astype(BF16), wd_ref[0], preferred_element_type=F32)


def _expert_ffn(xs, tile_expert, n_used, wg, wu, wd, *, tm, tf):
    n_rows = xs.shape[0]
    d_ff = wg.shape[2]
    grid_spec = pltpu.PrefetchScalarGridSpec(
        num_scalar_prefetch=2,
        grid=(n_rows // tm, d_ff // tf),
        in_specs=[
            pl.BlockSpec((tm, D_MODEL), lambda i, f, te, nu: (i, 0)),
            pl.BlockSpec((1, D_MODEL, tf), lambda i, f, te, nu: (te[i], 0, f)),
            pl.BlockSpec((1, D_MODEL, tf), lambda i, f, te, nu: (te[i], 0, f)),
            pl.BlockSpec((1, tf, D_MODEL), lambda i, f, te, nu: (te[i], f, 0)),
        ],
        out_specs=pl.BlockSpec((tm, D_MODEL), lambda i, f, te, nu: (i, 0)),
        scratch_shapes=[pltpu.VMEM((tm, D_MODEL), BF16)],
    )
    return pl.pallas_call(
        _expert_kernel,
        grid_spec=grid_spec,
        out_shape=jax.ShapeDtypeStruct((n_rows, D_MODEL), F32),
        compiler_params=_cparams("arbitrary", "arbitrary"),
        name="expert_ffn",
    )(tile_expert, n_used, xs, wg, wu, wd)


def _combine_kernel(yg_ref, gt_ref, x_ref, mod_ref, modc_ref, g_ref, b_ref, o_ref, *, tm):
    is_ctx = _row_is_ctx(tm, pl.program_id(1))
    gt = gt_ref[0]
    ffn = gt[:, 0:1] * yg_ref[0, :, 0:D_MODEL] + gt[:, 1:2] * yg_ref[0, :, D_MODEL:2 * D_MODEL]
    g2 = _mod_pick(is_ctx, modc_ref, mod_ref, 5)
    o_ref[0] = _layernorm_rows(DEEPNORM_ALPHA * x_ref[0] + (1.0 + g2) * ffn, g_ref[...], b_ref[...])


def _moe_combine(yg, gates, x1, mod, ln_g, ln_b, *, tm):
    b, rows, _ = x1.shape
    row_spec = lambda n: pl.BlockSpec((1, tm, n), lambda i, j: (i, j, 0))
    vec = lambda n: pl.BlockSpec((1, n), lambda i, j: (0, 0))
    return pl.pallas_call(
        functools.partial(_combine_kernel, tm=tm),
        grid=(b, rows // tm),
        in_specs=[
            row_spec(2 * D_MODEL), row_spec(LANES), row_spec(D_MODEL),
            pl.BlockSpec((1, 1, 6 * D_MODEL), lambda i, j: (i, 0, 0)),
            pl.BlockSpec((1, 1, 6 * D_MODEL), lambda i, j: (MOD_ROWS - 1, 0, 0)),
            vec(D_MODEL), vec(D_MODEL),
        ],
        out_specs=row_spec(D_MODEL),
        out_shape=jax.ShapeDtypeStruct((b, rows, D_MODEL), F32),
        compiler_params=_cparams("arbitrary", "arbitrary"),
        name="moe_combine",
    )(yg, gates, x1, mod, mod, ln_g.reshape(1, -1), ln_b.reshape(1, -1))


def _routing_plan(eidx, *, tm):
    t = eidx.shape[0]
    n_tiles = (t * TOP_K) // tm + N_EXPERTS
    flat = eidx.reshape(-1)
    onehot = (flat[:, None] == jnp.arange(N_EXPERTS, dtype=jnp.int32)[None, :]).astype(jnp.int32)
    csum = jnp.cumsum(onehot, axis=0)
    counts = csum[-1]
    rank = jnp.sum((csum - onehot) * onehot, axis=1)
    tiles = (counts + tm - 1) // tm
    tile_end = jnp.cumsum(tiles)
    tile_start = tile_end - tiles
    pos = jnp.sum(onehot * tile_start[None, :], axis=1) * tm + rank
    src = jnp.zeros((n_tiles * tm,), jnp.int32).at[pos].set(jnp.arange(t * TOP_K, dtype=jnp.int32) // TOP_K)
    n_used = tile_end[-1]
    tile_id = jnp.minimum(jnp.arange(n_tiles, dtype=jnp.int32), n_used - 1)
    tile_expert = jnp.sum((tile_id[:, None] >= tile_end[None, :]).astype(jnp.int32), axis=1)
    return pos.astype(jnp.int32), src, tile_expert.astype(jnp.int32), n_used.reshape(1).astype(jnp.int32)


def _moe_ffn(h2, eidx, gates, x1, mod, wg, wu, wd, ln_g, ln_b, *, tm_row, tm_expert, tf, rows_per_step):
    b, rows, _ = x1.shape
    t = b * rows
    pos, src, tile_expert, n_used = _routing_plan(eidx.reshape(t, LANES)[:, 0:TOP_K], tm=tm_expert)
    xs = _gather_rows(h2.reshape(t, D_MODEL), src, rows_per_step=rows_per_step)
    y = _expert_ffn(xs, tile_expert, n_used, wg, wu, wd, tm=tm_expert, tf=tf)
    yg = _gather_rows(y, pos, rows_per_step=rows_per_step).reshape(b, rows, TOP_K * D_MODEL)
    return _moe_combine(yg, gates, x1, mod, ln_g, ln_b, tm=tm_row)


TM_ROW = 768
TF_DENSE = 1408
TM_EXPERT = 1024
TF_EXPERT = 512
GATHER_ROWS = 512


def kernel(x, c, ctx, c_ctx, w_mod, b_mod, w_in, w_out, na_rpb, conv_w, conv_b, dt_bias, a_log, d_skip,
           ssd_norm_w, ln_g, ln_b, ffn_w_gate, ffn_w_up, ffn_w_down, router_w, moe_w_gate, moe_w_up,
           moe_w_down):
    b, seq, _ = x.shape
    rows_lat = seq // GRID_W
    xa = jnp.concatenate([ctx, x], axis=1)
    cvec = jnp.zeros((MOD_ROWS, D_MODEL), F32).at[0:b].set(c).at[MOD_ROWS - 1].set(c_ctx)
    mod_all = _modulation(cvec, w_mod, b_mod)
    cos_t, sin_t = _rope_tables(seq)
    sel = _head_select_matrix()
    w_in_p = jnp.pad(w_in, ((0, 0), (0, 0), (0, D_IN_PAD - D_IN_PROJ))).astype(BF16)
    w_out16 = w_out.astype(BF16)

    for layer in range(DEPTH):
        last = layer == DEPTH - 1
        is_moe = layer % 2 == 1
        mod = mod_all[layer].reshape(MOD_ROWS, 1, 6 * D_MODEL)
        qkv, z, xbc, dt = _in_proj(xa, mod, w_in_p[layer], tm=TM_ROW)
        a = _attention(qkv, _expand_rel_bias(na_rpb[layer], rows_lat), with_ctx_out=not last)
        y = _ssd_scan(xbc, dt, conv_w[layer], conv_b[layer], dt_bias[layer], a_log[layer], d_skip[layer],
                      cos_t, sin_t, sel)
        router_p = None
        if is_moe:
            router_p = jnp.pad(router_w[layer // 2], ((0, 0), (0, LANES - N_EXPERTS)))
        outs = _out_proj(a, y, z, xa, mod, w_out16[layer], ssd_norm_w[layer], ln_g[layer, 0], ln_b[layer, 0],
                         router_p, tm=TM_ROW, h2_dtype=F32 if is_moe else BF16)
        if is_moe:
            x1, h2, eidx, gates = outs
            m = layer // 2
            xa = _moe_ffn(h2, eidx, gates, x1, mod, moe_w_gate[m].astype(BF16), moe_w_up[m].astype(BF16),
                          moe_w_down[m].astype(BF16), ln_g[layer, 1], ln_b[layer, 1],
                          tm_row=TM_ROW, tm_expert=TM_EXPERT, tf=TF_EXPERT, rows_per_step=GATHER_ROWS)
        else:
            x1, h2 = outs
            m = layer // 2
            xa = _dense_ffn(h2, x1, mod, ffn_w_gate[m].astype(BF16), ffn_w_up[m].astype(BF16),
                            ffn_w_down[m].astype(BF16), ln_g[layer, 1], ln_b[layer, 1], tm=TM_ROW, tf=TF_DENSE)
    return xa[:, CTX_LEN:, :]
```

```python
import functools
import math

import numpy as np
import jax
import jax.numpy as jnp
from jax import lax
from jax.experimental import pallas as pl
from jax.experimental.pallas import tpu as pltpu
from jax.experimental.pallas import tpu_sc as plsc

F32 = jnp.float32
BF16 = jnp.bfloat16
HIGHEST = lax.Precision.HIGHEST

D_MODEL = 1024
DEPTH = 4
GRID_W = 64
CTX_LEN = 256
N_HEADS_NA = 8
HEAD_DIM_NA = 64
D_NA = N_HEADS_NA * HEAD_DIM_NA
WIN_R = 8
WIN_C = 16
N_HEADS_SSD = 8
HEAD_DIM_SSD = 64
D_SSD = N_HEADS_SSD * HEAD_DIM_SSD
N_GROUPS_SSD = 2
D_STATE = 128
D_CONV = 5
CHUNK = 128
ROPE_BASE = 10000.0
GN = N_GROUPS_SSD * D_STATE
CONV_CH = D_SSD + 2 * GN
D_IN_PROJ = 3 * D_NA + D_SSD + CONV_CH + 2 * N_HEADS_SSD
N_EXPERTS = 8
TOP_K = 2
DEEPNORM_ALPHA = (2.0 * DEPTH) ** 0.25
LN_EPS = 1e-5
RMS_EPS = 1e-5

LANES = 128
SUBLANES = 8
VMEM_LIMIT = 56 * 1024 * 1024
DT_PAD = LANES
D_IN_PAD = D_IN_PROJ - 2 * N_HEADS_SSD + DT_PAD
OFF_Z = 3 * D_NA
OFF_XBC = OFF_Z + D_SSD
OFF_DT = OFF_XBC + CONV_CH
MOD_ROWS = 24
HEAD_PAIR = 2 * HEAD_DIM_NA


def _cparams(*sem):
    return pltpu.CompilerParams(dimension_semantics=sem, vmem_limit_bytes=VMEM_LIMIT)


def _silu(v):
    return v * jax.nn.sigmoid(v)


def _layernorm_rows(v, g, b):
    mu = jnp.mean(v, axis=-1, keepdims=True)
    d = v - mu
    var = jnp.mean(d * d, axis=-1, keepdims=True)
    return d * lax.rsqrt(var + LN_EPS) * g + b


def _row_is_ctx(tm, j):
    row = lax.broadcasted_iota(jnp.int32, (tm, 1), 0) + j * tm
    return row < CTX_LEN


def _mod_pick(is_ctx, modc_ref, mod_ref, seg):
    lo, hi = seg * D_MODEL, (seg + 1) * D_MODEL
    return jnp.where(is_ctx, modc_ref[0, :, lo:hi], mod_ref[0, :, lo:hi])


def _mod_kernel(c_ref, w_ref, b_ref, o_ref):
    cs = _silu(c_ref[...])
    o_ref[0] = jnp.dot(cs, w_ref[0], preferred_element_type=F32, precision=HIGHEST) + b_ref[0]


def _modulation(cvec, w_mod, b_mod):
    tn = D_MODEL
    return pl.pallas_call(
        _mod_kernel,
        grid=(DEPTH, 6 * D_MODEL // tn),
        in_specs=[
            pl.BlockSpec((MOD_ROWS, D_MODEL), lambda l, n: (0, 0)),
            pl.BlockSpec((1, D_MODEL, tn), lambda l, n: (l, 0, n)),
            pl.BlockSpec((1, 1, tn), lambda l, n: (l, 0, n)),
        ],
        out_specs=pl.BlockSpec((1, MOD_ROWS, tn), lambda l, n: (l, 0, n)),
        out_shape=jax.ShapeDtypeStruct((DEPTH, MOD_ROWS, 6 * D_MODEL), F32),
        compiler_params=_cparams("arbitrary", "arbitrary"),
        name="modulation",
    )(cvec, w_mod, b_mod.reshape(DEPTH, 1, 6 * D_MODEL))


def _in_proj_kernel(x_ref, mod_ref, modc_ref, w_ref, qkv_ref, z_ref, xbc_ref, dt_ref, *, tm):
    is_ctx = _row_is_ctx(tm, pl.program_id(1))
    sh = _mod_pick(is_ctx, modc_ref, mod_ref, 0)
    sc = _mod_pick(is_ctx, modc_ref, mod_ref, 1)
    h = (x_ref[0] * (1.0 + sc) + sh).astype(BF16)
    qkv_ref[0] = jnp.dot(h, w_ref[:, 0:OFF_Z], preferred_element_type=F32).astype(BF16)
    z_ref[0] = jnp.dot(h, w_ref[:, OFF_Z:OFF_XBC], preferred_element_type=F32)
    xbc_ref[0] = jnp.dot(h, w_ref[:, OFF_XBC:OFF_DT], preferred_element_type=F32)
    dt_ref[0] = jnp.dot(h, w_ref[:, OFF_DT:D_IN_PAD], preferred_element_type=F32)


def _in_proj(xa, mod, w_in_p, *, tm):
    b, rows, _ = xa.shape
    row_spec = lambda n: pl.BlockSpec((1, tm, n), lambda i, j: (i, j, 0))
    return pl.pallas_call(
        functools.partial(_in_proj_kernel, tm=tm),
        grid=(b, rows // tm),
        in_specs=[
            row_spec(D_MODEL),
            pl.BlockSpec((1, 1, 6 * D_MODEL), lambda i, j: (i, 0, 0)),
            pl.BlockSpec((1, 1, 6 * D_MODEL), lambda i, j: (MOD_ROWS - 1, 0, 0)),
            pl.BlockSpec((D_MODEL, D_IN_PAD), lambda i, j: (0, 0)),
        ],
        out_specs=[row_spec(OFF_Z), row_spec(D_SSD), row_spec(CONV_CH), row_spec(DT_PAD)],
        out_shape=[
            jax.ShapeDtypeStruct((b, rows, OFF_Z), BF16),
            jax.ShapeDtypeStruct((b, rows, D_SSD), F32),
            jax.ShapeDtypeStruct((b, rows, CONV_CH), F32),
            jax.ShapeDtypeStruct((b, rows, DT_PAD), F32),
        ],
        compiler_params=_cparams("arbitrary", "arbitrary"),
        name="in_proj",
    )(xa, mod, mod, w_in_p)


def _attn_head_pair(q_pair, keys_vals, biases):
    m_rows = q_pair.shape[0]
    lane = lax.broadcasted_iota(jnp.int32, q_pair.shape, 1)
    zero = jnp.zeros_like(q_pair)
    q_stack = jnp.concatenate([jnp.where(lane < HEAD_DIM_NA, q_pair, zero),
                               jnp.where(lane >= HEAD_DIM_NA, q_pair, zero)], axis=0)
    scores = []
    for (k, _), bias in zip(keys_vals, biases):
        s = lax.dot_general(q_stack, k, (((1,), (1,)), ((), ())), preferred_element_type=F32)
        scores.append(s if bias is None else s + bias)
    m = scores[0].max(axis=-1, keepdims=True)
    for s in scores[1:]:
        m = jnp.maximum(m, s.max(axis=-1, keepdims=True))
    denom = jnp.zeros_like(m)
    acc = jnp.zeros((2 * m_rows, HEAD_PAIR), F32)
    for s, (_, v) in zip(scores, keys_vals):
        p = jnp.exp(s - m)
        denom = denom + p.sum(axis=-1, keepdims=True)
        acc = acc + jnp.dot(p.astype(BF16), v, preferred_element_type=F32)
    out = acc / denom
    return jnp.where(lane < HEAD_DIM_NA, out[0:m_rows], out[m_rows:2 * m_rows])


def _attn_kernel(qkv_ref, bias_ref, o_ref, *, rows_lat, with_ctx_out):
    kr = min(WIN_R, rows_lat)
    scale = HEAD_DIM_NA ** -0.5
    k_off, v_off = D_NA, 2 * D_NA
    n_pairs = N_HEADS_NA // 2

    def lat_row(i, carry):
        r0 = jnp.clip(i - kr // 2, 0, rows_lat - kr)
        d = i - r0
        q_base = pl.multiple_of(CTX_LEN + i * GRID_W, GRID_W)
        k_base = pl.multiple_of(CTX_LEN + r0 * GRID_W, GRID_W)
        for p in range(n_pairs):
            lo = p * HEAD_PAIR
            q = qkv_ref[0, pl.ds(q_base, GRID_W), lo:lo + HEAD_PAIR] * scale
            k_loc = qkv_ref[0, pl.ds(k_base, kr * GRID_W), k_off + lo:k_off + lo + HEAD_PAIR]
            v_loc = qkv_ref[0, pl.ds(k_base, kr * GRID_W), v_off + lo:v_off + lo + HEAD_PAIR]
            k_ctx = qkv_ref[0, 0:CTX_LEN, k_off + lo:k_off + lo + HEAD_PAIR]
            v_ctx = qkv_ref[0, 0:CTX_LEN, v_off + lo:v_off + lo + HEAD_PAIR]
            o = _attn_head_pair(q, [(k_loc, v_loc), (k_ctx, v_ctx)], [bias_ref[d, p], None])
            o_ref[0, pl.ds(q_base, GRID_W), lo:lo + HEAD_PAIR] = o.astype(o_ref.dtype)
        return carry

    lax.fori_loop(0, rows_lat, lat_row, 0)

    if with_ctx_out:
        for p in range(n_pairs):
            lo = p * HEAD_PAIR
            q = qkv_ref[0, 0:CTX_LEN, lo:lo + HEAD_PAIR] * scale
            k_ctx = qkv_ref[0, 0:CTX_LEN, k_off + lo:k_off + lo + HEAD_PAIR]
            v_ctx = qkv_ref[0, 0:CTX_LEN, v_off + lo:v_off + lo + HEAD_PAIR]
            o = _attn_head_pair(q, [(k_ctx, v_ctx)], [None])
            o_ref[0, 0:CTX_LEN, lo:lo + HEAD_PAIR] = o.astype(o_ref.dtype)
    else:
        o_ref[0, 0:CTX_LEN, :] = jnp.zeros((CTX_LEN, D_NA), o_ref.dtype)


def _attention(qkv, bias, *, with_ctx_out):
    b, rows, _ = qkv.shape
    rows_lat = (rows - CTX_LEN) // GRID_W
    return pl.pallas_call(
        functools.partial(_attn_kernel, rows_lat=rows_lat, with_ctx_out=with_ctx_out),
        grid=(b,),
        in_specs=[
            pl.BlockSpec((1, rows, OFF_Z), lambda i: (i, 0, 0)),
            pl.BlockSpec(bias.shape, lambda i: (0, 0, 0, 0)),
        ],
        out_specs=pl.BlockSpec((1, rows, D_NA), lambda i: (i, 0, 0)),
        out_shape=jax.ShapeDtypeStruct((b, rows, D_NA), BF16),
        compiler_params=_cparams("arbitrary"),
        name="attention",
    )(qkv, bias)


def _expand_rel_bias(rpb, rows_lat):
    kr = min(WIN_R, rows_lat)
    cols = np.arange(GRID_W)
    col_start = np.clip(cols - WIN_C // 2, 0, GRID_W - WIN_C)
    kc = cols[None, :]
    in_win = (kc >= col_start[:, None]) & (kc < col_start[:, None] + WIN_C)
    col_idx = np.clip(kc - cols[:, None] + WIN_C - 1, 0, 2 * WIN_C - 2)
    d = np.arange(WIN_R)[:, None]
    row_idx = np.clip(np.arange(kr)[None, :] - d + WIN_R - 1, 0, 2 * WIN_R - 2)
    t = rpb[:, row_idx]
    t = t[:, :, :, col_idx]
    t = jnp.where(jnp.asarray(in_win)[None, None, None], t.astype(F32), -jnp.inf)
    t = t.transpose(1, 0, 3, 2, 4)
    return t.reshape(WIN_R, N_HEADS_NA // 2, 2 * GRID_W, kr * GRID_W)


def _ssd_kernel(xbc_ref, dt_ref, convw_ref, convb_ref, dtb_ref, alog_ref, dskip_ref, cos_ref, sin_ref,
                sel_ref, y_ref, htf_ref, htb_ref, stf_ref, c_ref, ef_ref, *, n_chunks):
    n_ctx_chunks = CTX_LEN // CHUNK
    hp = D_SSD // N_GROUPS_SSD
    htf_ref[...] = jnp.zeros_like(htf_ref)
    htb_ref[...] = jnp.zeros_like(htb_ref)

    li = lax.broadcasted_iota(jnp.int32, (CHUNK, CHUNK), 0)
    si = lax.broadcasted_iota(jnp.int32, (CHUNK, CHUNK), 1)
    lower = li >= si
    upper = li <= si
    ltri = lower.astype(F32)
    utri = upper.astype(F32)
    lane_c = lax.broadcasted_iota(jnp.int32, (CHUNK, LANES), 1)
    rope_lo = (lane_c % (D_STATE // 2)) < (D_STATE // 4)
    pair_lane = lax.broadcasted_iota(jnp.int32, (CHUNK, HEAD_PAIR), 1)

    def rope(w, cos, sin):
        partner = jnp.where(rope_lo, pltpu.roll(w, LANES - D_STATE // 4, 1), pltpu.roll(w, D_STATE // 4, 1))
        return w * cos + partner * sin

    def local_pass(k, carry):
        c = jnp.where(k < n_ctx_chunks, n_ctx_chunks - 1 - k, n_chunks + n_ctx_chunks - 1 - k)
        base = pl.multiple_of(c * CHUNK, CHUNK)
        is_start = (c == 0) | (c == n_ctx_chunks)
        is_end = (c == n_ctx_chunks - 1) | (c == n_chunks - 1)
        cur = xbc_ref[0, pl.ds(base, CHUNK), :]
        prev_base = pl.multiple_of(jnp.maximum(base - SUBLANES, 0), SUBLANES)
        next_base = pl.multiple_of(jnp.minimum(base + CHUNK, (n_chunks - 1) * CHUNK + CHUNK - SUBLANES), SUBLANES)
        prev8 = jnp.where(is_start, 0.0, xbc_ref[0, pl.ds(prev_base, SUBLANES), :])
        next8 = jnp.where(is_end, 0.0, xbc_ref[0, pl.ds(next_base, SUBLANES), :])
        ext = jnp.concatenate([prev8, cur, next8], axis=0)
        conv = convb_ref[...]
        for t in range(D_CONV):
            lo = SUBLANES - D_CONV // 2 + t
            conv = conv + ext[lo:lo + CHUNK] * convw_ref[t:t + 1, :]
        u = _silu(conv)
        xs = u[:, 0:D_SSD]
        cos = cos_ref[pl.ds(base, CHUNK), :]
        sin = sin_ref[pl.ds(base, CHUNK), :]
        bm = [rope(u[:, D_SSD + g * D_STATE:D_SSD + (g + 1) * D_STATE], cos, sin) for g in range(N_GROUPS_SSD)]
        cm = [rope(u[:, D_SSD + GN + g * D_STATE:D_SSD + GN + (g + 1) * D_STATE], cos, sin)
              for g in range(N_GROUPS_SSD)]

        dt = jax.nn.softplus(dt_ref[0, pl.ds(base, CHUNK), :] + dtb_ref[...])
        a = dt * (-jnp.exp(alog_ref[...]))
        pre = jnp.dot(ltri, a, preferred_element_type=F32, precision=HIGHEST)
        suf = jnp.dot(utri, a, preferred_element_type=F32, precision=HIGHEST)
        acs = jnp.where(lane_c < N_HEADS_SSD, pre, suf)
        acs_t = acs.T
        dt_t = dt.T
        acs_x = jnp.dot(acs, sel_ref[...], preferred_element_type=F32, precision=HIGHEST)
        dt_x = jnp.dot(dt, sel_ref[...], preferred_element_type=F32, precision=HIGHEST)
        arg_f, arg_b = acs_x[:, 0:D_SSD], acs_x[:, D_SSD:2 * D_SSD]
        e_f = jnp.exp(arg_f)
        e_b = jnp.exp(arg_b)
        x_f = (xs * dt_x[:, 0:D_SSD] * jnp.exp(arg_f[CHUNK - 1:CHUNK, :] - arg_f)).astype(BF16)
        x_b = (xs * dt_x[:, D_SSD:2 * D_SSD] * jnp.exp(arg_b[0:1, :] - arg_b)).astype(BF16)

        xs16 = xs.astype(BF16)
        y_parts = []
        for g in range(N_GROUPS_SSD):
            bt = bm[g].T.astype(BF16)
            c16 = cm[g].astype(BF16)
            gmat = jnp.dot(c16, bt, preferred_element_type=F32)
            gs = slice(g * hp, (g + 1) * hp)
            heads = range(g * (N_HEADS_SSD // N_GROUPS_SSD), (g + 1) * (N_HEADS_SSD // N_GROUPS_SSD))
            ms = []
            for h in heads:
                jf, jb = h, N_HEADS_SSD + h
                wf = jnp.where(lower, jnp.exp(acs[:, jf:jf + 1] - acs_t[jf:jf + 1, :]), 0.0) * dt_t[jf:jf + 1, :]
                wb = jnp.where(upper, jnp.exp(acs[:, jb:jb + 1] - acs_t[jb:jb + 1, :]), 0.0) * dt_t[jb:jb + 1, :]
                ms.append((gmat * (wf + wb)).astype(BF16))
            y_g = []
            for q in range(len(ms) // 2):
                lo = g * hp + q * HEAD_PAIR
                xp = xs16[:, lo:lo + HEAD_PAIR]
                rhs = jnp.concatenate([jnp.where(pair_lane < HEAD_DIM_SSD, xp, jnp.zeros_like(xp)),
                                       jnp.where(pair_lane >= HEAD_DIM_SSD, xp, jnp.zeros_like(xp))], axis=0)
                lhs = jnp.concatenate([ms[2 * q], ms[2 * q + 1]], axis=1)
                y_g.append(jnp.dot(lhs, rhs, preferred_element_type=F32))
            y_diag = jnp.concatenate(y_g, axis=1)
            y_off = jnp.dot(c16, htb_ref[g].astype(BF16), preferred_element_type=F32) * e_b[:, gs]
            y_parts.append(y_diag + y_off)
            htb_ref[g] = htb_ref[g] * e_b[0:1, gs] + jnp.dot(bt, x_b[:, gs], preferred_element_type=F32)
            stf_ref[c, g] = jnp.dot(bt, x_f[:, gs], preferred_element_type=F32)
            c_ref[c, g] = c16
        ef_ref[c] = e_f
        y_ref[0, pl.ds(base, CHUNK), :] = jnp.concatenate(y_parts, axis=1) + dskip_ref[...] * xs
        return carry

    lax.fori_loop(0, n_chunks, local_pass, 0)

    def forward_pass(c, carry):
        base = pl.multiple_of(c * CHUNK, CHUNK)
        e_f = ef_ref[c]
        y_parts = []
        for g in range(N_GROUPS_SSD):
            gs = slice(g * hp, (g + 1) * hp)
            y_parts.append(jnp.dot(c_ref[c, g], htf_ref[g].astype(BF16), preferred_element_type=F32) * e_f[:, gs])
            htf_ref[g] = htf_ref[g] * e_f[CHUNK - 1:CHUNK, gs] + stf_ref[c, g]
        y_ref[0, pl.ds(base, CHUNK), :] = y_ref[0, pl.ds(base, CHUNK), :] + jnp.concatenate(y_parts, axis=1)
        return carry

    lax.fori_loop(0, n_chunks, forward_pass, 0)


def _ssd_scan(xbc, dt, conv_w, conv_b, dt_bias, a_log, d_skip, cos_t, sin_t, sel):
    b, rows, _ = xbc.shape
    n_chunks = rows // CHUNK
    hp = D_SSD // N_GROUPS_SSD
    full = lambda a: pl.BlockSpec(a.shape, lambda i: (0,) * a.ndim)
    pad16 = lambda v: jnp.pad(v.reshape(1, -1), ((0, 0), (0, DT_PAD - v.size)))
    conv_w8 = jnp.pad(conv_w, ((0, SUBLANES - D_CONV), (0, 0)))
    params = [conv_w8, conv_b.reshape(1, -1), pad16(dt_bias), pad16(a_log),
              jnp.repeat(d_skip, HEAD_DIM_SSD).reshape(1, -1), cos_t, sin_t, sel]
    return pl.pallas_call(
        functools.partial(_ssd_kernel, n_chunks=n_chunks),
        grid=(b,),
        in_specs=[pl.BlockSpec((1, rows, CONV_CH), lambda i: (i, 0, 0)),
                  pl.BlockSpec((1, rows, DT_PAD), lambda i: (i, 0, 0))] + [full(p) for p in params],
        out_specs=pl.BlockSpec((1, rows, D_SSD), lambda i: (i, 0, 0)),
        out_shape=jax.ShapeDtypeStruct((b, rows, D_SSD), F32),
        scratch_shapes=[
            pltpu.VMEM((N_GROUPS_SSD, D_STATE, hp), F32),
            pltpu.VMEM((N_GROUPS_SSD, D_STATE, hp), F32),
            pltpu.VMEM((n_chunks, N_GROUPS_SSD, D_STATE, hp), F32),
            pltpu.VMEM((n_chunks, N_GROUPS_SSD, CHUNK, D_STATE), BF16),
            pltpu.VMEM((n_chunks, CHUNK, D_SSD), F32),
        ],
        compiler_params=_cparams("arbitrary"),
        name="ssd_scan",
    )(xbc, dt, *params)


def _rope_tables(seq):
    half, quarter = D_STATE // 2, D_STATE // 4
    t = jnp.arange(seq)
    inv_freq = ROPE_BASE ** (-jnp.arange(quarter, dtype=F32) / quarter)
    ang_r = (t // GRID_W).astype(F32)[:, None] * inv_freq
    ang_c = (t % GRID_W).astype(F32)[:, None] * inv_freq
    cos = jnp.concatenate([jnp.cos(ang_r)] * 2 + [jnp.cos(ang_c)] * 2, axis=-1)
    sin = jnp.concatenate([-jnp.sin(ang_r), jnp.sin(ang_r), -jnp.sin(ang_c), jnp.sin(ang_c)], axis=-1)
    cos = jnp.concatenate([jnp.ones((CTX_LEN, 2 * half), F32), cos], axis=0)
    sin = jnp.concatenate([jnp.zeros((CTX_LEN, 2 * half), F32), sin], axis=0)
    return cos, sin


def _head_select_matrix():
    sel = np.zeros((DT_PAD, 2 * D_SSD), np.float32)
    for j in range(2 * N_HEADS_SSD):
        sel[j, j * HEAD_DIM_SSD:(j + 1) * HEAD_DIM_SSD] = 1.0
    return jnp.asarray(sel)


def _out_proj_kernel(*refs, tm, with_router):
    if with_router:
        (a_ref, y_ref, z_ref, x_ref, mod_ref, modc_ref, w_ref, nw_ref, g_ref, b_ref, r_ref,
         x1_ref, h2_ref, ei_ref, gt_ref) = refs
    else:
        a_ref, y_ref, z_ref, x_ref, mod_ref, modc_ref, w_ref, nw_ref, g_ref, b_ref, x1_ref, h2_ref = refs
    is_ctx = _row_is_ctx(tm, pl.program_id(1))
    yg = y_ref[0] * _silu(z_ref[0])
    gw = D_SSD // N_GROUPS_SSD
    parts = []
    for g in range(N_GROUPS_SSD):
        part = yg[:, g * gw:(g + 1) * gw]
        parts.append(part * lax.rsqrt(jnp.mean(part * part, axis=-1, keepdims=True) + RMS_EPS))
    s = (jnp.concatenate(parts, axis=1) * nw_ref[...]).astype(BF16)
    mix = (jnp.dot(a_ref[0], w_ref[0:D_NA, :], preferred_element_type=F32)
           + jnp.dot(s, w_ref[D_NA:D_NA + D_SSD, :], preferred_element_type=F32))
    g1 = _mod_pick(is_ctx, modc_ref, mod_ref, 2)
    x1 = _layernorm_rows(DEEPNORM_ALPHA * x_ref[0] + (1.0 + g1) * mix, g_ref[...], b_ref[...])
    x1_ref[0] = x1
    sh2 = _mod_pick(is_ctx, modc_ref, mod_ref, 3)
    sc2 = _mod_pick(is_ctx, modc_ref, mod_ref, 4)
    h2 = x1 * (1.0 + sc2) + sh2
    h2_ref[0] = h2.astype(h2_ref.dtype)
    if with_router:
        logits = jnp.dot(h2, r_ref[...], preferred_element_type=F32, precision=HIGHEST)
        lane = lax.broadcasted_iota(jnp.int32, logits.shape, 1)
        logits = jnp.where(lane < N_EXPERTS, logits, -jnp.inf)
        m1 = logits.max(axis=-1, keepdims=True)
        i1 = jnp.where(logits == m1, lane, LANES).min(axis=-1, keepdims=True)
        rest = jnp.where(lane == i1, -jnp.inf, logits)
        m2 = rest.max(axis=-1, keepdims=True)
        i2 = jnp.where(rest == m2, lane, LANES).min(axis=-1, keepdims=True)
        e2 = jnp.exp(m2 - m1)
        gate1 = 1.0 / (1.0 + e2)
        gate2 = e2 / (1.0 + e2)
        ei_ref[0] = jnp.where(lane == 0, i1, jnp.where(lane == 1, i2, 0))
        gt_ref[0] = jnp.where(lane == 0, gate1, jnp.where(lane == 1, gate2, 0.0))


def _out_proj(a, y, z, xa, mod, w_out, norm_w, ln_g, ln_b, router_p, *, tm, h2_dtype):
    b, rows, _ = xa.shape
    with_router = router_p is not None
    row_spec = lambda n: pl.BlockSpec((1, tm, n), lambda i, j: (i, j, 0))
    vec = lambda n: pl.BlockSpec((1, n), lambda i, j: (0, 0))
    in_specs = [
        row_spec(D_NA), row_spec(D_SSD), row_spec(D_SSD), row_spec(D_MODEL),
        pl.BlockSpec((1, 1, 6 * D_MODEL), lambda i, j: (i, 0, 0)),
        pl.BlockSpec((1, 1, 6 * D_MODEL), lambda i, j: (MOD_ROWS - 1, 0, 0)),
        pl.BlockSpec((D_NA + D_SSD, D_MODEL), lambda i, j: (0, 0)),
        vec(D_SSD), vec(D_MODEL), vec(D_MODEL),
    ]
    args = [a, y, z, xa, mod, mod, w_out, norm_w.reshape(1, -1), ln_g.reshape(1, -1), ln_b.reshape(1, -1)]
    out_specs = [row_spec(D_MODEL), row_spec(D_MODEL)]
    out_shape = [jax.ShapeDtypeStruct((b, rows, D_MODEL), F32), jax.ShapeDtypeStruct((b, rows, D_MODEL), h2_dtype)]
    if with_router:
        in_specs.append(pl.BlockSpec((D_MODEL, LANES), lambda i, j: (0, 0)))
        args.append(router_p)
        out_specs += [row_spec(LANES), row_spec(LANES)]
        out_shape += [jax.ShapeDtypeStruct((b, rows, LANES), jnp.int32), jax.ShapeDtypeStruct((b, rows, LANES), F32)]
    return pl.pallas_call(
        functools.partial(_out_proj_kernel, tm=tm, with_router=with_router),
        grid=(b, rows // tm),
        in_specs=in_specs,
        out_specs=out_specs,
        out_shape=out_shape,
        compiler_params=_cparams("arbitrary", "arbitrary"),
        name="out_proj",
    )(*args)


def _ffn_kernel(h_ref, x_ref, mod_ref, modc_ref, wg_ref, wu_ref, wd_ref, g_ref, b_ref, o_ref, *, tm, tf):
    is_ctx = _row_is_ctx(tm, pl.program_id(1))
    h = h_ref[0]
    d_ff = wg_ref.shape[1]
    acc = jnp.zeros((tm, D_MODEL), F32)
    for f in range(d_ff // tf):
        fs = slice(f * tf, (f + 1) * tf)
        gate = jnp.dot(h, wg_ref[:, fs], preferred_element_type=F32)
        up = jnp.dot(h, wu_ref[:, fs], preferred_element_type=F32)
        acc = acc + jnp.dot((_silu(gate) * up).astype(BF16), wd_ref[fs, :], preferred_element_type=F32)
    g2 = _mod_pick(is_ctx, modc_ref, mod_ref, 5)
    o_ref[0] = _layernorm_rows(DEEPNORM_ALPHA * x_ref[0] + (1.0 + g2) * acc, g_ref[...], b_ref[...])


def _dense_ffn(h2, x1, mod, wg, wu, wd, ln_g, ln_b, *, tm, tf):
    b, rows, _ = x1.shape
    d_ff = wg.shape[1]
    row_spec = lambda n: pl.BlockSpec((1, tm, n), lambda i, j: (i, j, 0))
    vec = lambda n: pl.BlockSpec((1, n), lambda i, j: (0, 0))
    const = lambda s: pl.BlockSpec(s, lambda i, j: (0, 0), pipeline_mode=pl.Buffered(1))
    return pl.pallas_call(
        functools.partial(_ffn_kernel, tm=tm, tf=tf),
        grid=(b, rows // tm),
        in_specs=[
            row_spec(D_MODEL), row_spec(D_MODEL),
            pl.BlockSpec((1, 1, 6 * D_MODEL), lambda i, j: (i, 0, 0)),
            pl.BlockSpec((1, 1, 6 * D_MODEL), lambda i, j: (MOD_ROWS - 1, 0, 0)),
            const((D_MODEL, d_ff)), const((D_MODEL, d_ff)), const((d_ff, D_MODEL)),
            vec(D_MODEL), vec(D_MODEL),
        ],
        out_specs=row_spec(D_MODEL),
        out_shape=jax.ShapeDtypeStruct((b, rows, D_MODEL), F32),
        compiler_params=_cparams("arbitrary", "arbitrary"),
        name="dense_ffn",
    )(h2, x1, mod, mod, wg, wu, wd, ln_g.reshape(1, -1), ln_b.reshape(1, -1))


def _gather_rows(table, idx):
    n, d = idx.shape[0], table.shape[1]
    workers = SC_CORES * SC_SUBCORES
    n_chunks = n // (workers * GATHER_CHUNK)
    assert n == workers * n_chunks * GATHER_CHUNK
    per_worker = n_chunks * GATHER_CHUNK
    mesh = plsc.VectorSubcoreMesh(core_axis_name="c", subcore_axis_name="s")

    def body(table_hbm, idx_hbm, out_hbm, idx_v, rows_v, sem):
        wid = lax.axis_index("s") * SC_CORES + lax.axis_index("c")
        pltpu.sync_copy(idx_hbm.at[wid], idx_v)
        base = wid * per_worker

        @pl.loop(0, n_chunks)
        def _(j):
            pltpu.async_copy(table_hbm.at[idx_v.at[j]], rows_v, sem).wait()
            pltpu.sync_copy(rows_v, out_hbm.at[pl.ds(base + j * GATHER_CHUNK, GATHER_CHUNK)])

    return pl.kernel(
        body,
        out_type=jax.ShapeDtypeStruct((n, d), table.dtype),
        mesh=mesh,
        scratch_types=[
            pltpu.VMEM((n_chunks, GATHER_CHUNK), jnp.int32),
            pltpu.VMEM((GATHER_CHUNK, d), table.dtype),
            pltpu.SemaphoreType.DMA,
        ],
        name="gather_rows",
    )(table, idx.reshape(workers, n_chunks, GATHER_CHUNK))


def _expert_kernel(te_ref, nu_ref, x_ref, wg_ref, wu_ref, wd_ref, y_ref, x16_ref):
    i, f = pl.program_id(0), pl.program_id(1)
    live = i < nu_ref[0]

    @pl.when(f == 0)
    def _():
        y_ref[...] = jnp.zeros_like(y_ref)
        x16_ref[...] = x_ref[...].astype(BF16)

    @pl.when(live)
    def _():
        h = x16_ref[...]
        gate = jnp.dot(h, wg_ref[0], preferred_element_type=F32)
        up = jnp.dot(h, wu_ref[0], preferred_element_type=F32)
        y_ref[...] += jnp.dot((_silu(gate) * up).astype(BF16), wd_ref[0], preferred_element_type=F32)


def _expert_ffn(xs, tile_expert, n_used, wg, wu, wd, *, tm, tf):
    n_rows = xs.shape[0]
    d_ff = wg.shape[2]
    grid_spec = pltpu.PrefetchScalarGridSpec(
        num_scalar_prefetch=2,
        grid=(n_rows // tm, d_ff // tf),
        in_specs=[
            pl.BlockSpec((tm, D_MODEL), lambda i, f, te, nu: (i, 0)),
            pl.BlockSpec((1, D_MODEL, tf), lambda i, f, te, nu: (te[i], 0, f)),
            pl.BlockSpec((1, D_MODEL, tf), lambda i, f, te, nu: (te[i], 0, f)),
            pl.BlockSpec((1, tf, D_MODEL), lambda i, f, te, nu: (te[i], f, 0)),
        ],
        out_specs=pl.BlockSpec((tm, D_MODEL), lambda i, f, te, nu: (i, 0)),
        scratch_shapes=[pltpu.VMEM((tm, D_MODEL), BF16)],
    )
    return pl.pallas_call(
        _expert_kernel,
        grid_spec=grid_spec,
        out_shape=jax.ShapeDtypeStruct((n_rows, D_MODEL), F32),
        compiler_params=_cparams("arbitrary", "arbitrary"),
        name="expert_ffn",
    )(tile_expert, n_used, xs, wg, wu, wd)


def _combine_kernel(y0_ref, y1_ref, gt_ref, x_ref, mod_ref, modc_ref, g_ref, b_ref, o_ref, *, tm):
    is_ctx = _row_is_ctx(tm, pl.program_id(1))
    gt = gt_ref[0]
    ffn = gt[:, 0:1] * y0_ref[0, 0] + gt[:, 1:2] * y1_ref[0, 0]
    g2 = _mod_pick(is_ctx, modc_ref, mod_ref, 5)
    o_ref[0] = _layernorm_rows(DEEPNORM_ALPHA * x_ref[0] + (1.0 + g2) * ffn, g_ref[...], b_ref[...])


def _moe_combine(yg, gates, x1, mod, ln_g, ln_b, *, tm):
    b, rows, _ = x1.shape
    row_spec = lambda n: pl.BlockSpec((1, tm, n), lambda i, j: (i, j, 0))
    vec = lambda n: pl.BlockSpec((1, n), lambda i, j: (0, 0))
    return pl.pallas_call(
        functools.partial(_combine_kernel, tm=tm),
        grid=(b, rows // tm),
        in_specs=[
            pl.BlockSpec((1, 1, tm, D_MODEL), lambda i, j: (0, i, j, 0)),
            pl.BlockSpec((1, 1, tm, D_MODEL), lambda i, j: (1, i, j, 0)),
            row_spec(LANES), row_spec(D_MODEL),
            pl.BlockSpec((1, 1, 6 * D_MODEL), lambda i, j: (i, 0, 0)),
            pl.BlockSpec((1, 1, 6 * D_MODEL), lambda i, j: (MOD_ROWS - 1, 0, 0)),
            vec(D_MODEL), vec(D_MODEL),
        ],
        out_specs=row_spec(D_MODEL),
        out_shape=jax.ShapeDtypeStruct((b, rows, D_MODEL), F32),
        compiler_params=_cparams("arbitrary", "arbitrary"),
        name="moe_combine",
    )(yg, yg, gates, x1, mod, mod, ln_g.reshape(1, -1), ln_b.reshape(1, -1))


def _routing_plan(eidx, *, tm):
    t = eidx.shape[0]
    n_tiles = (t * TOP_K) // tm + N_EXPERTS
    flat = eidx.reshape(-1)
    onehot = (flat[:, None] == jnp.arange(N_EXPERTS, dtype=jnp.int32)[None, :]).astype(jnp.int32)
    csum = jnp.cumsum(onehot, axis=0)
    counts = csum[-1]
    rank = jnp.sum((csum - onehot) * onehot, axis=1)
    tiles = (counts + tm - 1) // tm
    tile_end = jnp.cumsum(tiles)
    tile_start = tile_end - tiles
    pos = jnp.sum(onehot * tile_start[None, :], axis=1) * tm + rank
    src = jnp.zeros((n_tiles * tm,), jnp.int32).at[pos].set(jnp.arange(t * TOP_K, dtype=jnp.int32) // TOP_K)
    n_used = tile_end[-1]
    tile_id = jnp.minimum(jnp.arange(n_tiles, dtype=jnp.int32), n_used - 1)
    tile_expert = jnp.sum((tile_id[:, None] >= tile_end[None, :]).astype(jnp.int32), axis=1)
    return pos.astype(jnp.int32), src, tile_expert.astype(jnp.int32), n_used.reshape(1).astype(jnp.int32)


def _moe_ffn(h2, eidx, gates, x1, mod, wg, wu, wd, ln_g, ln_b, *, tm_row, tm_expert, tf):
    b, rows, _ = x1.shape
    t = b * rows
    pos, src, tile_expert, n_used = _routing_plan(eidx.reshape(t, LANES)[:, 0:TOP_K], tm=tm_expert)
    xs = _gather_rows(h2.reshape(t, D_MODEL), src)
    y = _expert_ffn(xs, tile_expert, n_used, wg, wu, wd, tm=tm_expert, tf=tf)
    yg = _gather_rows(y, pos.reshape(t, TOP_K).T.reshape(-1)).reshape(TOP_K, b, rows, D_MODEL)
    return _moe_combine(yg, gates, x1, mod, ln_g, ln_b, tm=tm_row)


TM_ROW = 768
TF_DENSE = 1408
TM_EXPERT = 1024
TF_EXPERT = 512
SC_CORES = 2
SC_SUBCORES = 16
GATHER_CHUNK = 32


def kernel(x, c, ctx, c_ctx, w_mod, b_mod, w_in, w_out, na_rpb, conv_w, conv_b, dt_bias, a_log, d_skip,
           ssd_norm_w, ln_g, ln_b, ffn_w_gate, ffn_w_up, ffn_w_down, router_w, moe_w_gate, moe_w_up,
           moe_w_down):
    b, seq, _ = x.shape
    rows_lat = seq // GRID_W
    xa = jnp.concatenate([ctx, x], axis=1)
    cvec = jnp.zeros((MOD_ROWS, D_MODEL), F32).at[0:b].set(c).at[MOD_ROWS - 1].set(c_ctx)
    mod_all = _modulation(cvec, w_mod, b_mod)
    cos_t, sin_t = _rope_tables(seq)
    sel = _head_select_matrix()
    w_in_p = jnp.pad(w_in, ((0, 0), (0, 0), (0, D_IN_PAD - D_IN_PROJ))).astype(BF16)
    w_out16 = w_out.astype(BF16)

    for layer in range(DEPTH):
        last = layer == DEPTH - 1
        is_moe = layer % 2 == 1
        mod = mod_all[layer].reshape(MOD_ROWS, 1, 6 * D_MODEL)
        qkv, z, xbc, dt = _in_proj(xa, mod, w_in_p[layer], tm=TM_ROW)
        a = _attention(qkv, _expand_rel_bias(na_rpb[layer], rows_lat), with_ctx_out=not last)
        y = _ssd_scan(xbc, dt, conv_w[layer], conv_b[layer], dt_bias[layer], a_log[layer], d_skip[layer],
                      cos_t, sin_t, sel)
        router_p = None
        if is_moe:
            router_p = jnp.pad(router_w[layer // 2], ((0, 0), (0, LANES - N_EXPERTS)))
        outs = _out_proj(a, y, z, xa, mod, w_out16[layer], ssd_norm_w[layer], ln_g[layer, 0], ln_b[layer, 0],
                         router_p, tm=TM_ROW, h2_dtype=F32 if is_moe else BF16)
        if is_moe:
            x1, h2, eidx, gates = outs
            m = layer // 2
            xa = _moe_ffn(h2, eidx, gates, x1, mod, moe_w_gate[m].astype(BF16), moe_w_up[m].astype(BF16),
                          moe_w_down[m].astype(BF16), ln_g[layer, 1], ln_b[layer, 1],
                          tm_row=TM_ROW, tm_expert=TM_EXPERT, tf=TF_EXPERT)
        else:
            x1, h2 = outs
            m = layer // 2
            xa = _dense_ffn(h2, x1, mod, ffn_w_gate[m].astype(BF16), ffn_w_up[m].astype(BF16),
                            ffn_w_down[m].astype(BF16), ln_g[layer, 1], ln_b[layer, 1], tm=TM_ROW, tf=TF_DENSE)
    return xa[:, CTX_LEN:, :]
```

```python
import functools
import math

import numpy as np
import jax
import jax.numpy as jnp
from jax import lax
from jax.experimental import pallas as pl
from jax.experimental.pallas import tpu as pltpu
from jax.experimental.pallas import tpu_sc as plsc

F32 = jnp.float32
BF16 = jnp.bfloat16
HIGHEST = lax.Precision.HIGHEST

D_MODEL = 1024
DEPTH = 4
GRID_W = 64
CTX_LEN = 256
N_HEADS_NA = 8
HEAD_DIM_NA = 64
D_NA = N_HEADS_NA * HEAD_DIM_NA
WIN_R = 8
WIN_C = 16
N_HEADS_SSD = 8
HEAD_DIM_SSD = 64
D_SSD = N_HEADS_SSD * HEAD_DIM_SSD
N_GROUPS_SSD = 2
D_STATE = 128
D_CONV = 5
CHUNK = 128
ROPE_BASE = 10000.0
GN = N_GROUPS_SSD * D_STATE
CONV_CH = D_SSD + 2 * GN
D_IN_PROJ = 3 * D_NA + D_SSD + CONV_CH + 2 * N_HEADS_SSD
N_EXPERTS = 8
TOP_K = 2
DEEPNORM_ALPHA = (2.0 * DEPTH) ** 0.25
LN_EPS = 1e-5
RMS_EPS = 1e-5

LANES = 128
SUBLANES = 8
VMEM_LIMIT = 56 * 1024 * 1024
DT_PAD = LANES
D_IN_PAD = D_IN_PROJ - 2 * N_HEADS_SSD + DT_PAD
OFF_Z = 3 * D_NA
OFF_XBC = OFF_Z + D_SSD
OFF_DT = OFF_XBC + CONV_CH
MOD_ROWS = 24
HEAD_PAIR = 2 * HEAD_DIM_NA
ROWS_PER_STEP = 2
WIN_ROWS = WIN_R + ROWS_PER_STEP - 1


def _cparams(*sem):
    return pltpu.CompilerParams(dimension_semantics=sem, vmem_limit_bytes=VMEM_LIMIT)


def _silu(v):
    return v * jax.nn.sigmoid(v)


def _layernorm_rows(v, g, b):
    mu = jnp.mean(v, axis=-1, keepdims=True)
    d = v - mu
    var = jnp.mean(d * d, axis=-1, keepdims=True)
    return d * lax.rsqrt(var + LN_EPS) * g + b


def _row_is_ctx(tm, j, row0=0):
    row = lax.broadcasted_iota(jnp.int32, (tm, 1), 0) + j * tm + row0
    return row < CTX_LEN


def _mod_pick(is_ctx, modc_ref, mod_ref, seg):
    lo, hi = seg * D_MODEL, (seg + 1) * D_MODEL
    return jnp.where(is_ctx, modc_ref[0, :, lo:hi], mod_ref[0, :, lo:hi])


def _mod_kernel(c_ref, w_ref, b_ref, o_ref):
    cs = _silu(c_ref[...])
    o_ref[0] = jnp.dot(cs, w_ref[0], preferred_element_type=F32, precision=HIGHEST) + b_ref[0]


def _modulation(cvec, w_mod, b_mod):
    tn = D_MODEL
    return pl.pallas_call(
        _mod_kernel,
        grid=(DEPTH, 6 * D_MODEL // tn),
        in_specs=[
            pl.BlockSpec((MOD_ROWS, D_MODEL), lambda l, n: (0, 0)),
            pl.BlockSpec((1, D_MODEL, tn), lambda l, n: (l, 0, n)),
            pl.BlockSpec((1, 1, tn), lambda l, n: (l, 0, n)),
        ],
        out_specs=pl.BlockSpec((1, MOD_ROWS, tn), lambda l, n: (l, 0, n)),
        out_shape=jax.ShapeDtypeStruct((DEPTH, MOD_ROWS, 6 * D_MODEL), F32),
        compiler_params=_cparams("arbitrary", "arbitrary"),
        name="modulation",
    )(cvec, w_mod, b_mod.reshape(DEPTH, 1, 6 * D_MODEL))


def _in_proj_kernel(x_ref, mod_ref, modc_ref, w_ref, qkv_ref, z_ref, xbc_ref, dt_ref, *, tm):
    is_ctx = _row_is_ctx(tm, pl.program_id(1))
    sh = _mod_pick(is_ctx, modc_ref, mod_ref, 0)
    sc = _mod_pick(is_ctx, modc_ref, mod_ref, 1)
    h = (x_ref[0] * (1.0 + sc) + sh).astype(BF16)
    qkv_ref[0] = jnp.dot(h, w_ref[:, 0:OFF_Z], preferred_element_type=F32).astype(BF16)
    z_ref[0] = jnp.dot(h, w_ref[:, OFF_Z:OFF_XBC], preferred_element_type=F32)
    xbc_ref[0] = jnp.dot(h, w_ref[:, OFF_XBC:OFF_DT], preferred_element_type=F32)
    dt_ref[0] = jnp.dot(h, w_ref[:, OFF_DT:D_IN_PAD], preferred_element_type=F32)


def _in_proj(xa, mod, w_in_p, *, tm):
    b, rows, _ = xa.shape
    row_spec = lambda n: pl.BlockSpec((1, tm, n), lambda i, j: (i, j, 0))
    return pl.pallas_call(
        functools.partial(_in_proj_kernel, tm=tm),
        grid=(b, rows // tm),
        in_specs=[
            row_spec(D_MODEL),
            pl.BlockSpec((1, 1, 6 * D_MODEL), lambda i, j: (i, 0, 0)),
            pl.BlockSpec((1, 1, 6 * D_MODEL), lambda i, j: (MOD_ROWS - 1, 0, 0)),
            pl.BlockSpec((D_MODEL, D_IN_PAD), lambda i, j: (0, 0)),
        ],
        out_specs=[row_spec(OFF_Z), row_spec(D_SSD), row_spec(CONV_CH), row_spec(DT_PAD)],
        out_shape=[
            jax.ShapeDtypeStruct((b, rows, OFF_Z), BF16),
            jax.ShapeDtypeStruct((b, rows, D_SSD), F32),
            jax.ShapeDtypeStruct((b, rows, CONV_CH), F32),
            jax.ShapeDtypeStruct((b, rows, DT_PAD), F32),
        ],
        compiler_params=_cparams("arbitrary", "arbitrary"),
        name="in_proj",
    )(xa, mod, mod, w_in_p)


def _attn_head_pair(q_pair, keys_vals, biases):
    m_rows = q_pair.shape[0]
    lane = lax.broadcasted_iota(jnp.int32, q_pair.shape, 1)
    zero = jnp.zeros_like(q_pair)
    q_stack = jnp.concatenate([jnp.where(lane < HEAD_DIM_NA, q_pair, zero),
                               jnp.where(lane >= HEAD_DIM_NA, q_pair, zero)], axis=0)
    scores = []
    for (k, _), bias in zip(keys_vals, biases):
        s = lax.dot_general(q_stack, k, (((1,), (1,)), ((), ())), preferred_element_type=F32)
        scores.append(s if bias is None else s + bias)
    m = scores[0].max(axis=-1, keepdims=True)
    for s in scores[1:]:
        m = jnp.maximum(m, s.max(axis=-1, keepdims=True))
    denom = jnp.zeros_like(m)
    acc = jnp.zeros((2 * m_rows, HEAD_PAIR), F32)
    for s, (_, v) in zip(scores, keys_vals):
        p = jnp.exp(s - m)
        denom = denom + p.sum(axis=-1, keepdims=True)
        acc = acc + jnp.dot(p.astype(BF16), v, preferred_element_type=F32)
    out = acc / denom
    return jnp.where(lane < HEAD_DIM_NA, out[0:m_rows], out[m_rows:2 * m_rows])


def _attn_kernel(base_ref, cfg_ref, qkv_ref, bias_ref, o_ref, *, rows_lat, with_ctx_out):
    scale = HEAD_DIM_NA ** -0.5
    k_off, v_off = D_NA, 2 * D_NA
    n_pairs = N_HEADS_NA // 2
    q_rows = ROWS_PER_STEP * GRID_W
    win = WIN_ROWS * GRID_W

    def lat_rows(j, carry):
        q_base = pl.multiple_of(CTX_LEN + j * q_rows, q_rows)
        k_base = pl.multiple_of(CTX_LEN + base_ref[j] * GRID_W, GRID_W)
        cfg = cfg_ref[j]
        for p in range(n_pairs):
            lo = p * HEAD_PAIR
            q = qkv_ref[0, pl.ds(q_base, q_rows), lo:lo + HEAD_PAIR] * scale
            k_loc = qkv_ref[0, pl.ds(k_base, win), k_off + lo:k_off + lo + HEAD_PAIR]
            v_loc = qkv_ref[0, pl.ds(k_base, win), v_off + lo:v_off + lo + HEAD_PAIR]
            k_ctx = qkv_ref[0, 0:CTX_LEN, k_off + lo:k_off + lo + HEAD_PAIR]
            v_ctx = qkv_ref[0, 0:CTX_LEN, v_off + lo:v_off + lo + HEAD_PAIR]
            o = _attn_head_pair(q, [(k_loc, v_loc), (k_ctx, v_ctx)], [bias_ref[cfg, p], None])
            o_ref[0, pl.ds(q_base, q_rows), lo:lo + HEAD_PAIR] = o.astype(o_ref.dtype)
        return carry

    lax.fori_loop(0, rows_lat // ROWS_PER_STEP, lat_rows, 0)

    if with_ctx_out:
        for p in range(n_pairs):
            lo = p * HEAD_PAIR
            q = qkv_ref[0, 0:CTX_LEN, lo:lo + HEAD_PAIR] * scale
            k_ctx = qkv_ref[0, 0:CTX_LEN, k_off + lo:k_off + lo + HEAD_PAIR]
            v_ctx = qkv_ref[0, 0:CTX_LEN, v_off + lo:v_off + lo + HEAD_PAIR]
            o = _attn_head_pair(q, [(k_ctx, v_ctx)], [None])
            o_ref[0, 0:CTX_LEN, lo:lo + HEAD_PAIR] = o.astype(o_ref.dtype)
    else:
        o_ref[0, 0:CTX_LEN, :] = jnp.zeros((CTX_LEN, D_NA), o_ref.dtype)


def _attention(qkv, bias, win_base, cfg, *, with_ctx_out):
    b, rows, _ = qkv.shape
    rows_lat = (rows - CTX_LEN) // GRID_W
    grid_spec = pltpu.PrefetchScalarGridSpec(
        num_scalar_prefetch=2,
        grid=(b,),
        in_specs=[
            pl.BlockSpec((1, rows, OFF_Z), lambda i, wb, cf: (i, 0, 0)),
            pl.BlockSpec(bias.shape, lambda i, wb, cf: (0, 0, 0, 0)),
        ],
        out_specs=pl.BlockSpec((1, rows, D_NA), lambda i, wb, cf: (i, 0, 0)),
    )
    return pl.pallas_call(
        functools.partial(_attn_kernel, rows_lat=rows_lat, with_ctx_out=with_ctx_out),
        grid_spec=grid_spec,
        out_shape=jax.ShapeDtypeStruct((b, rows, D_NA), BF16),
        compiler_params=_cparams("arbitrary"),
        name="attention",
    )(win_base, cfg, qkv, bias)


def _attn_window_plan(rows_lat):
    kr = min(WIN_R, rows_lat)
    assert rows_lat % ROWS_PER_STEP == 0 and rows_lat >= WIN_ROWS and kr == WIN_R
    bases, cfg_ids, configs = [], [], []
    for i in range(0, rows_lat, ROWS_PER_STEP):
        r0 = [int(np.clip(i + s - kr // 2, 0, rows_lat - kr)) for s in range(ROWS_PER_STEP)]
        base = min(r0[0], rows_lat - WIN_ROWS)
        assert all(0 <= r - base and r - base + kr <= WIN_ROWS for r in r0)
        c = tuple((i + s - r0[s], r0[s] - base) for s in range(ROWS_PER_STEP))
        if c not in configs:
            configs.append(c)
        bases.append(base)
        cfg_ids.append(configs.index(c))
    return np.asarray(bases, np.int32), np.asarray(cfg_ids, np.int32), configs


def _expand_rel_bias(rpb, configs):
    cols = np.arange(GRID_W)
    col_start = np.clip(cols - WIN_C // 2, 0, GRID_W - WIN_C)
    kc = cols[None, :]
    in_win = (kc >= col_start[:, None]) & (kc < col_start[:, None] + WIN_C)
    col_idx = np.clip(kc - cols[:, None] + WIN_C - 1, 0, 2 * WIN_C - 2)
    d = np.asarray([[c[s][0] for s in range(ROWS_PER_STEP)] for c in configs])[:, :, None]
    o = np.asarray([[c[s][1] for s in range(ROWS_PER_STEP)] for c in configs])[:, :, None]
    r = np.arange(WIN_ROWS)[None, None, :] - o
    row_ok = (r >= 0) & (r < WIN_R)
    row_idx = np.clip(r - d + WIN_R - 1, 0, 2 * WIN_R - 2)
    t = rpb[:, row_idx]
    t = t[..., col_idx]
    ok = row_ok[None, :, :, :, None, None] & in_win[None, None, None, None]
    t = jnp.where(jnp.asarray(ok), t.astype(F32), -jnp.inf)
    t = t.transpose(1, 0, 2, 4, 3, 5)
    return t.reshape(len(configs), N_HEADS_NA // 2, 2 * ROWS_PER_STEP * GRID_W, WIN_ROWS * GRID_W)


def _split3(v):
    hi = v.astype(BF16)
    rest = v - hi.astype(F32)
    mid = rest.astype(BF16)
    lo = (rest - mid.astype(F32)).astype(BF16)
    return hi, mid, lo


def _ssd_kernel(xbc_ref, dt_ref, convw_ref, convb_ref, dtb_ref, alog_ref, dskip_ref, cos_ref, sin_ref,
                sel_ref, y_ref, htf_ref, htb_ref, stf_ref, c_ref, ef_ref, ext_ref, *, n_chunks):
    n_ctx_chunks = CTX_LEN // CHUNK
    hp = D_SSD // N_GROUPS_SSD
    htf_ref[...] = jnp.zeros_like(htf_ref)
    htb_ref[...] = jnp.zeros_like(htb_ref)

    li = lax.broadcasted_iota(jnp.int32, (CHUNK, CHUNK), 0)
    si = lax.broadcasted_iota(jnp.int32, (CHUNK, CHUNK), 1)
    lower = li >= si
    upper = li <= si
    tri = jnp.concatenate([lower, upper], axis=0).astype(F32).astype(BF16)
    lane_c = lax.broadcasted_iota(jnp.int32, (CHUNK, LANES), 1)
    rope_lo = (lane_c % (D_STATE // 2)) < (D_STATE // 4)
    pair_lane = lax.broadcasted_iota(jnp.int32, (CHUNK, HEAD_PAIR), 1)

    def rope(w, cos, sin):
        partner = jnp.where(rope_lo, pltpu.roll(w, LANES - D_STATE // 4, 1), pltpu.roll(w, D_STATE // 4, 1))
        return w * cos + partner * sin

    def local_pass(k, carry):
        c = jnp.where(k < n_ctx_chunks, n_ctx_chunks - 1 - k, n_chunks + n_ctx_chunks - 1 - k)
        base = pl.multiple_of(c * CHUNK, CHUNK)
        is_start = (c == 0) | (c == n_ctx_chunks)
        is_end = (c == n_ctx_chunks - 1) | (c == n_chunks - 1)
        prev_base = pl.multiple_of(jnp.maximum(base - SUBLANES, 0), SUBLANES)
        next_base = pl.multiple_of(jnp.minimum(base + CHUNK, (n_chunks - 1) * CHUNK + CHUNK - SUBLANES), SUBLANES)
        ext_ref[0:SUBLANES, :] = jnp.where(is_start, 0.0, xbc_ref[0, pl.ds(prev_base, SUBLANES), :])
        ext_ref[SUBLANES:SUBLANES + CHUNK, :] = xbc_ref[0, pl.ds(base, CHUNK), :]
        ext_ref[SUBLANES + CHUNK:2 * SUBLANES + CHUNK, :] = jnp.where(
            is_end, 0.0, xbc_ref[0, pl.ds(next_base, SUBLANES), :])
        ext = ext_ref[...]
        conv = convb_ref[...]
        for t in range(D_CONV):
            shift = (D_CONV // 2 - t) % ext.shape[0]
            tap = ext if shift == 0 else pltpu.roll(ext, shift, 0)
            conv = conv + tap[SUBLANES:SUBLANES + CHUNK] * convw_ref[t:t + 1, :]
        u = _silu(conv)
        xs = u[:, 0:D_SSD]
        cos = cos_ref[pl.ds(base, CHUNK), :]
        sin = sin_ref[pl.ds(base, CHUNK), :]
        bm = [rope(u[:, D_SSD + g * D_STATE:D_SSD + (g + 1) * D_STATE], cos, sin) for g in range(N_GROUPS_SSD)]
        cm = [rope(u[:, D_SSD + GN + g * D_STATE:D_SSD + GN + (g + 1) * D_STATE], cos, sin)
              for g in range(N_GROUPS_SSD)]

        dt = jax.nn.softplus(dt_ref[0, pl.ds(base, CHUNK), :] + dtb_ref[...])
        a = dt * (-jnp.exp(alog_ref[...]))
        sums = jnp.dot(tri, jnp.concatenate(_split3(a), axis=1), preferred_element_type=F32)
        sums = sums[:, 0:LANES] + sums[:, LANES:2 * LANES] + sums[:, 2 * LANES:3 * LANES]
        acs = jnp.where(lane_c < N_HEADS_SSD, sums[0:CHUNK], sums[CHUNK:2 * CHUNK])
        acs_t = acs.T
        dt_t = dt.T
        wide = jnp.dot(jnp.concatenate(_split3(jnp.concatenate([acs, dt], axis=0)), axis=1), sel_ref[...],
                       preferred_element_type=F32)
        acs_x, dt_x = wide[0:CHUNK], wide[CHUNK:2 * CHUNK]
        arg_f, arg_b = acs_x[:, 0:D_SSD], acs_x[:, D_SSD:2 * D_SSD]
        e_f = jnp.exp(arg_f)
        e_b = jnp.exp(arg_b)
        x_f = (xs * dt_x[:, 0:D_SSD] * jnp.exp(arg_f[CHUNK - 1:CHUNK, :] - arg_f)).astype(BF16)
        x_b = (xs * dt_x[:, D_SSD:2 * D_SSD] * jnp.exp(arg_b[0:1, :] - arg_b)).astype(BF16)

        xs16 = xs.astype(BF16)
        y_parts = []
        for g in range(N_GROUPS_SSD):
            bt = bm[g].T.astype(BF16)
            c16 = cm[g].astype(BF16)
            gmat = jnp.dot(c16, bt, preferred_element_type=F32)
            gs = slice(g * hp, (g + 1) * hp)
            heads = range(g * (N_HEADS_SSD // N_GROUPS_SSD), (g + 1) * (N_HEADS_SSD // N_GROUPS_SSD))
            ms = []
            for h in heads:
                jf, jb = h, N_HEADS_SSD + h
                wf = jnp.where(lower, jnp.exp(acs[:, jf:jf + 1] - acs_t[jf:jf + 1, :]), 0.0) * dt_t[jf:jf + 1, :]
                wb = jnp.where(upper, jnp.exp(acs[:, jb:jb + 1] - acs_t[jb:jb + 1, :]), 0.0) * dt_t[jb:jb + 1, :]
                ms.append((gmat * (wf + wb)).astype(BF16))
            y_g = []
            for q in range(len(ms) // 2):
                lo = g * hp + q * HEAD_PAIR
                xp = xs16[:, lo:lo + HEAD_PAIR]
                rhs = jnp.concatenate([jnp.where(pair_lane < HEAD_DIM_SSD, xp, jnp.zeros_like(xp)),
                                       jnp.where(pair_lane >= HEAD_DIM_SSD, xp, jnp.zeros_like(xp))], axis=0)
                lhs = jnp.concatenate([ms[2 * q], ms[2 * q + 1]], axis=1)
                y_g.append(jnp.dot(lhs, rhs, preferred_element_type=F32))
            y_diag = jnp.concatenate(y_g, axis=1)
            y_off = jnp.dot(c16, htb_ref[g].astype(BF16), preferred_element_type=F32) * e_b[:, gs]
            y_parts.append(y_diag + y_off)
            htb_ref[g] = htb_ref[g] * e_b[0:1, gs] + jnp.dot(bt, x_b[:, gs], preferred_element_type=F32)
            stf_ref[c, g] = jnp.dot(bt, x_f[:, gs], preferred_element_type=F32)
            c_ref[c, g] = c16
        ef_ref[c] = e_f
        y_ref[0, pl.ds(base, CHUNK), :] = jnp.concatenate(y_parts, axis=1) + dskip_ref[...] * xs
        return carry

    lax.fori_loop(0, n_chunks, local_pass, 0)

    def forward_pass(c, carry):
        base = pl.multiple_of(c * CHUNK, CHUNK)
        e_f = ef_ref[c]
        y_parts = []
        for g in range(N_GROUPS_SSD):
            gs = slice(g * hp, (g + 1) * hp)
            y_parts.append(jnp.dot(c_ref[c, g], htf_ref[g].astype(BF16), preferred_element_type=F32) * e_f[:, gs])
            htf_ref[g] = htf_ref[g] * e_f[CHUNK - 1:CHUNK, gs] + stf_ref[c, g]
        y_ref[0, pl.ds(base, CHUNK), :] = y_ref[0, pl.ds(base, CHUNK), :] + jnp.concatenate(y_parts, axis=1)
        return carry

    lax.fori_loop(0, n_chunks, forward_pass, 0)


def _ssd_scan(xbc, dt, conv_w, conv_b, dt_bias, a_log, d_skip, cos_t, sin_t, sel):
    b, rows, _ = xbc.shape
    n_chunks = rows // CHUNK
    hp = D_SSD // N_GROUPS_SSD
    full = lambda a: pl.BlockSpec(a.shape, lambda i: (0,) * a.ndim)
    pad16 = lambda v: jnp.pad(v.reshape(1, -1), ((0, 0), (0, DT_PAD - v.size)))
    conv_w8 = jnp.pad(conv_w, ((0, SUBLANES - D_CONV), (0, 0)))
    params = [conv_w8, conv_b.reshape(1, -1), pad16(dt_bias), pad16(a_log),
              jnp.repeat(d_skip, HEAD_DIM_SSD).reshape(1, -1), cos_t, sin_t, sel]
    return pl.pallas_call(
        functools.partial(_ssd_kernel, n_chunks=n_chunks),
        grid=(b,),
        in_specs=[pl.BlockSpec((1, rows, CONV_CH), lambda i: (i, 0, 0)),
                  pl.BlockSpec((1, rows, DT_PAD), lambda i: (i, 0, 0))] + [full(p) for p in params],
        out_specs=pl.BlockSpec((1, rows, D_SSD), lambda i: (i, 0, 0)),
        out_shape=jax.ShapeDtypeStruct((b, rows, D_SSD), F32),
        scratch_shapes=[
            pltpu.VMEM((N_GROUPS_SSD, D_STATE, hp), F32),
            pltpu.VMEM((N_GROUPS_SSD, D_STATE, hp), F32),
            pltpu.VMEM((n_chunks, N_GROUPS_SSD, D_STATE, hp), F32),
            pltpu.VMEM((n_chunks, N_GROUPS_SSD, CHUNK, D_STATE), BF16),
            pltpu.VMEM((n_chunks, CHUNK, D_SSD), F32),
            pltpu.VMEM((CHUNK + 2 * SUBLANES, CONV_CH), F32),
        ],
        compiler_params=_cparams("arbitrary"),
        name="ssd_scan",
    )(xbc, dt, *params)


def _rope_tables(seq):
    half, quarter = D_STATE // 2, D_STATE // 4
    t = jnp.arange(seq)
    inv_freq = ROPE_BASE ** (-jnp.arange(quarter, dtype=F32) / quarter)
    ang_r = (t // GRID_W).astype(F32)[:, None] * inv_freq
    ang_c = (t % GRID_W).astype(F32)[:, None] * inv_freq
    cos = jnp.concatenate([jnp.cos(ang_r)] * 2 + [jnp.cos(ang_c)] * 2, axis=-1)
    sin = jnp.concatenate([-jnp.sin(ang_r), jnp.sin(ang_r), -jnp.sin(ang_c), jnp.sin(ang_c)], axis=-1)
    cos = jnp.concatenate([jnp.ones((CTX_LEN, 2 * half), F32), cos], axis=0)
    sin = jnp.concatenate([jnp.zeros((CTX_LEN, 2 * half), F32), sin], axis=0)
    return cos, sin


def _head_select_matrix():
    sel = np.zeros((DT_PAD, 2 * D_SSD), np.float32)
    for j in range(2 * N_HEADS_SSD):
        sel[j, j * HEAD_DIM_SSD:(j + 1) * HEAD_DIM_SSD] = 1.0
    return jnp.asarray(np.concatenate([sel] * 3, axis=0), dtype=BF16)


def _out_proj_kernel(*refs, tm, with_router, row0):
    if with_router:
        (a_ref, y_ref, z_ref, x_ref, mod_ref, modc_ref, w_ref, nw_ref, g_ref, b_ref, r_ref,
         x1_ref, h2_ref, ei_ref, gt_ref) = refs
    else:
        a_ref, y_ref, z_ref, x_ref, mod_ref, modc_ref, w_ref, nw_ref, g_ref, b_ref, x1_ref, h2_ref = refs
    is_ctx = _row_is_ctx(tm, pl.program_id(1), row0)
    yg = y_ref[0] * _silu(z_ref[0])
    gw = D_SSD // N_GROUPS_SSD
    parts = []
    for g in range(N_GROUPS_SSD):
        part = yg[:, g * gw:(g + 1) * gw]
        parts.append(part * lax.rsqrt(jnp.mean(part * part, axis=-1, keepdims=True) + RMS_EPS))
    s = (jnp.concatenate(parts, axis=1) * nw_ref[...]).astype(BF16)
    mix = (jnp.dot(a_ref[0], w_ref[0:D_NA, :], preferred_element_type=F32)
           + jnp.dot(s, w_ref[D_NA:D_NA + D_SSD, :], preferred_element_type=F32))
    g1 = _mod_pick(is_ctx, modc_ref, mod_ref, 2)
    x1 = _layernorm_rows(DEEPNORM_ALPHA * x_ref[0] + (1.0 + g1) * mix, g_ref[...], b_ref[...])
    x1_ref[0] = x1
    sh2 = _mod_pick(is_ctx, modc_ref, mod_ref, 3)
    sc2 = _mod_pick(is_ctx, modc_ref, mod_ref, 4)
    h2 = x1 * (1.0 + sc2) + sh2
    h2_ref[0] = h2.astype(h2_ref.dtype)
    if with_router:
        h_hi = h2.astype(BF16)
        h_mid = (h2 - h_hi.astype(F32)).astype(BF16)
        logits = jnp.dot(jnp.concatenate([h_hi, h_hi, h_mid], axis=1), r_ref[...],
                         preferred_element_type=F32)
        lane = lax.broadcasted_iota(jnp.int32, logits.shape, 1)
        logits = jnp.where(lane < N_EXPERTS, logits, -jnp.inf)
        m1 = logits.max(axis=-1, keepdims=True)
        i1 = jnp.where(logits == m1, lane, LANES).min(axis=-1, keepdims=True)
        rest = jnp.where(lane == i1, -jnp.inf, logits)
        m2 = rest.max(axis=-1, keepdims=True)
        i2 = jnp.where(rest == m2, lane, LANES).min(axis=-1, keepdims=True)
        e2 = jnp.exp(m2 - m1)
        gate1 = 1.0 / (1.0 + e2)
        gate2 = e2 / (1.0 + e2)
        ei_ref[0] = jnp.where(lane == 0, i1, jnp.where(lane == 1, i2, 0))
        gt_ref[0] = jnp.where(lane == 0, gate1, jnp.where(lane == 1, gate2, 0.0))


def _out_proj(a, y, z, xa, mod, w_out, norm_w, ln_g, ln_b, router_p, *, tm, h2_dtype, row0=0):
    b, rows_in, _ = xa.shape
    rows = rows_in - row0
    assert row0 % tm == 0 and rows % tm == 0
    with_router = router_p is not None
    row_spec = lambda n: pl.BlockSpec((1, tm, n), lambda i, j: (i, j, 0))
    in_row_spec = lambda n: pl.BlockSpec((1, tm, n), lambda i, j: (i, j + row0 // tm, 0))
    vec = lambda n: pl.BlockSpec((1, n), lambda i, j: (0, 0))
    in_specs = [
        in_row_spec(D_NA), in_row_spec(D_SSD), in_row_spec(D_SSD), in_row_spec(D_MODEL),
        pl.BlockSpec((1, 1, 6 * D_MODEL), lambda i, j: (i, 0, 0)),
        pl.BlockSpec((1, 1, 6 * D_MODEL), lambda i, j: (MOD_ROWS - 1, 0, 0)),
        pl.BlockSpec((D_NA + D_SSD, D_MODEL), lambda i, j: (0, 0)),
        vec(D_SSD), vec(D_MODEL), vec(D_MODEL),
    ]
    args = [a, y, z, xa, mod, mod, w_out, norm_w.reshape(1, -1), ln_g.reshape(1, -1), ln_b.reshape(1, -1)]
    out_specs = [row_spec(D_MODEL), row_spec(D_MODEL)]
    out_shape = [jax.ShapeDtypeStruct((b, rows, D_MODEL), F32), jax.ShapeDtypeStruct((b, rows, D_MODEL), h2_dtype)]
    if with_router:
        in_specs.append(pl.BlockSpec((3 * D_MODEL, LANES), lambda i, j: (0, 0)))
        args.append(router_p)
        out_specs += [row_spec(LANES), row_spec(LANES)]
        out_shape += [jax.ShapeDtypeStruct((b, rows, LANES), jnp.int32), jax.ShapeDtypeStruct((b, rows, LANES), F32)]
    return pl.pallas_call(
        functools.partial(_out_proj_kernel, tm=tm, with_router=with_router, row0=row0),
        grid=(b, rows // tm),
        in_specs=in_specs,
        out_specs=out_specs,
        out_shape=out_shape,
        compiler_params=_cparams("arbitrary", "arbitrary"),
        name="out_proj",
    )(*args)


def _ffn_kernel(h_ref, x_ref, mod_ref, modc_ref, wg_ref, wu_ref, wd_ref, g_ref, b_ref, o_ref, *, tm, tf):
    is_ctx = _row_is_ctx(tm, pl.program_id(1))
    h = h_ref[0]
    d_ff = wg_ref.shape[1]
    acc = jnp.zeros((tm, D_MODEL), F32)
    for f in range(d_ff // tf):
        fs = slice(f * tf, (f + 1) * tf)
        gate = jnp.dot(h, wg_ref[:, fs], preferred_element_type=F32)
        up = jnp.dot(h, wu_ref[:, fs], preferred_element_type=F32)
        acc = acc + jnp.dot((_silu(gate) * up).astype(BF16), wd_ref[fs, :], preferred_element_type=F32)
    g2 = _mod_pick(is_ctx, modc_ref, mod_ref, 5)
    o_ref[0] = _layernorm_rows(DEEPNORM_ALPHA * x_ref[0] + (1.0 + g2) * acc, g_ref[...], b_ref[...])


def _dense_ffn(h2, x1, mod, wg, wu, wd, ln_g, ln_b, *, tm, tf):
    b, rows, _ = x1.shape
    d_ff = wg.shape[1]
    row_spec = lambda n: pl.BlockSpec((1, tm, n), lambda i, j: (i, j, 0))
    vec = lambda n: pl.BlockSpec((1, n), lambda i, j: (0, 0))
    const = lambda s: pl.BlockSpec(s, lambda i, j: (0, 0), pipeline_mode=pl.Buffered(1))
    return pl.pallas_call(
        functools.partial(_ffn_kernel, tm=tm, tf=tf),
        grid=(b, rows // tm),
        in_specs=[
            row_spec(D_MODEL), row_spec(D_MODEL),
            pl.BlockSpec((1, 1, 6 * D_MODEL), lambda i, j: (i, 0, 0)),
            pl.BlockSpec((1, 1, 6 * D_MODEL), lambda i, j: (MOD_ROWS - 1, 0, 0)),
            const((D_MODEL, d_ff)), const((D_MODEL, d_ff)), const((d_ff, D_MODEL)),
            vec(D_MODEL), vec(D_MODEL),
        ],
        out_specs=row_spec(D_MODEL),
        out_shape=jax.ShapeDtypeStruct((b, rows, D_MODEL), F32),
        compiler_params=_cparams("arbitrary", "arbitrary"),
        name="dense_ffn",
    )(h2, x1, mod, mod, wg, wu, wd, ln_g.reshape(1, -1), ln_b.reshape(1, -1))


def _gather_rows(table, idx):
    n, d = idx.shape[0], table.shape[1]
    workers = SC_CORES * SC_SUBCORES
    n_chunks = n // (workers * GATHER_CHUNK)
    assert n == workers * n_chunks * GATHER_CHUNK
    per_worker = n_chunks * GATHER_CHUNK
    mesh = plsc.VectorSubcoreMesh(core_axis_name="c", subcore_axis_name="s")

    def body(table_hbm, idx_hbm, out_hbm, idx_v, rows_v, sem):
        wid = lax.axis_index("s") * SC_CORES + lax.axis_index("c")
        pltpu.sync_copy(idx_hbm.at[wid], idx_v)
        base = wid * per_worker

        @pl.loop(0, n_chunks)
        def _(j):
            pltpu.async_copy(table_hbm.at[idx_v.at[j]], rows_v, sem).wait()
            pltpu.sync_copy(rows_v, out_hbm.at[pl.ds(base + j * GATHER_CHUNK, GATHER_CHUNK)])

    return pl.kernel(
        body,
        out_type=jax.ShapeDtypeStruct((n, d), table.dtype),
        mesh=mesh,
        scratch_types=[
            pltpu.VMEM((n_chunks, GATHER_CHUNK), jnp.int32),
            pltpu.VMEM((GATHER_CHUNK, d), table.dtype),
            pltpu.SemaphoreType.DMA,
        ],
        name="gather_rows",
    )(table, idx.reshape(workers, n_chunks, GATHER_CHUNK))


def _expert_kernel(te_ref, nu_ref, x_ref, wg_ref, wu_ref, wd_ref, y_ref, x16_ref):
    i, f = pl.program_id(0), pl.program_id(1)
    live = i < nu_ref[0]

    @pl.when(f == 0)
    def _():
        y_ref[...] = jnp.zeros_like(y_ref)
        x16_ref[...] = x_ref[...].astype(BF16)

    @pl.when(live)
    def _():
        h = x16_ref[...]
        gate = jnp.dot(h, wg_ref[0].astype(BF16), preferred_element_type=F32)
        up = jnp.dot(h, wu_ref[0].astype(BF16), preferred_element_type=F32)
        y_ref[...] += jnp.dot((_silu(gate) * up).astype(BF16), wd_ref[0].astype(BF16),
                              preferred_element_type=F32)


def _expert_ffn(xs, tile_expert, n_used, wg, wu, wd, *, tm, tf):
    n_rows = xs.shape[0]
    d_ff = wg.shape[2]
    n_f = d_ff // tf
    f_blk = lambda i, f, nu: jnp.where(i < nu[0], f, n_f - 1)
    grid_spec = pltpu.PrefetchScalarGridSpec(
        num_scalar_prefetch=2,
        grid=(n_rows // tm, n_f),
        in_specs=[
            pl.BlockSpec((tm, D_MODEL), lambda i, f, te, nu: (i, 0)),
            pl.BlockSpec((1, D_MODEL, tf), lambda i, f, te, nu: (te[i], 0, f_blk(i, f, nu))),
            pl.BlockSpec((1, D_MODEL, tf), lambda i, f, te, nu: (te[i], 0, f_blk(i, f, nu))),
            pl.BlockSpec((1, tf, D_MODEL), lambda i, f, te, nu: (te[i], f_blk(i, f, nu), 0)),
        ],
        out_specs=pl.BlockSpec((tm, D_MODEL), lambda i, f, te, nu: (i, 0)),
        scratch_shapes=[pltpu.VMEM((tm, D_MODEL), BF16)],
    )
    return pl.pallas_call(
        _expert_kernel,
        grid_spec=grid_spec,
        out_shape=jax.ShapeDtypeStruct((n_rows, D_MODEL), F32),
        compiler_params=_cparams("arbitrary", "arbitrary"),
        name="expert_ffn",
    )(tile_expert, n_used, xs, wg, wu, wd)


def _combine_kernel(y0_ref, y1_ref, gt_ref, x_ref, mod_ref, modc_ref, g_ref, b_ref, o_ref, *, tm, row0):
    is_ctx = _row_is_ctx(tm, pl.program_id(1), row0)
    gt = gt_ref[0]
    ffn = gt[:, 0:1] * y0_ref[0, 0] + gt[:, 1:2] * y1_ref[0, 0]
    g2 = _mod_pick(is_ctx, modc_ref, mod_ref, 5)
    o_ref[0] = _layernorm_rows(DEEPNORM_ALPHA * x_ref[0] + (1.0 + g2) * ffn, g_ref[...], b_ref[...])


def _moe_combine(yg, gates, x1, mod, ln_g, ln_b, *, tm, row0):
    b, rows, _ = x1.shape
    row_spec = lambda n: pl.BlockSpec((1, tm, n), lambda i, j: (i, j, 0))
    vec = lambda n: pl.BlockSpec((1, n), lambda i, j: (0, 0))
    return pl.pallas_call(
        functools.partial(_combine_kernel, tm=tm, row0=row0),
        grid=(b, rows // tm),
        in_specs=[
            pl.BlockSpec((1, 1, tm, D_MODEL), lambda i, j: (0, i, j, 0)),
            pl.BlockSpec((1, 1, tm, D_MODEL), lambda i, j: (1, i, j, 0)),
            row_spec(LANES), row_spec(D_MODEL),
            pl.BlockSpec((1, 1, 6 * D_MODEL), lambda i, j: (i, 0, 0)),
            pl.BlockSpec((1, 1, 6 * D_MODEL), lambda i, j: (MOD_ROWS - 1, 0, 0)),
            vec(D_MODEL), vec(D_MODEL),
        ],
        out_specs=row_spec(D_MODEL),
        out_shape=jax.ShapeDtypeStruct((b, rows, D_MODEL), F32),
        compiler_params=_cparams("arbitrary", "arbitrary"),
        name="moe_combine",
    )(yg, yg, gates, x1, mod, mod, ln_g.reshape(1, -1), ln_b.reshape(1, -1))


def _routing_plan(eidx, *, tm):
    t = eidx.shape[0]
    n_tiles = (t * TOP_K) // tm + N_EXPERTS
    flat = eidx.reshape(-1)
    onehot = (flat[:, None] == jnp.arange(N_EXPERTS, dtype=jnp.int32)[None, :]).astype(jnp.int32)
    csum = jnp.cumsum(onehot, axis=0)
    counts = csum[-1]
    rank = jnp.sum((csum - onehot) * onehot, axis=1)
    tiles = (counts + tm - 1) // tm
    tile_end = jnp.cumsum(tiles)
    tile_start = tile_end - tiles
    pos = jnp.sum(onehot * tile_start[None, :], axis=1) * tm + rank
    src = jnp.zeros((n_tiles * tm,), jnp.int32).at[pos].set(jnp.arange(t * TOP_K, dtype=jnp.int32) // TOP_K)
    n_used = tile_end[-1]
    tile_id = jnp.minimum(jnp.arange(n_tiles, dtype=jnp.int32), n_used - 1)
    tile_expert = jnp.sum((tile_id[:, None] >= tile_end[None, :]).astype(jnp.int32), axis=1)
    return pos.astype(jnp.int32), src, tile_expert.astype(jnp.int32), n_used.reshape(1).astype(jnp.int32)


def _moe_ffn(h2, eidx, gates, x1, mod, wg, wu, wd, ln_g, ln_b, *, first_expert, row0, tm_row, tm_expert, tf):
    b, rows, _ = x1.shape
    t = b * rows
    pos, src, tile_expert, n_used = _routing_plan(eidx.reshape(t, LANES)[:, 0:TOP_K], tm=tm_expert)
    xs = _gather_rows(h2.reshape(t, D_MODEL), src)
    y = _expert_ffn(xs, tile_expert + first_expert, n_used, wg, wu, wd, tm=tm_expert, tf=tf)
    yg = _gather_rows(y, pos.reshape(t, TOP_K).T.reshape(-1)).reshape(TOP_K, b, rows, D_MODEL)
    return _moe_combine(yg, gates, x1, mod, ln_g, ln_b, tm=tm_row, row0=row0)


TM_ROW = 768
TM_LAST = 256
TF_DENSE = 1408
TM_EXPERT = 1024
TF_EXPERT = 512
SC_CORES = 2
SC_SUBCORES = 16
GATHER_CHUNK = 32


def kernel(x, c, ctx, c_ctx, w_mod, b_mod, w_in, w_out, na_rpb, conv_w, conv_b, dt_bias, a_log, d_skip,
           ssd_norm_w, ln_g, ln_b, ffn_w_gate, ffn_w_up, ffn_w_down, router_w, moe_w_gate, moe_w_up,
           moe_w_down):
    b, seq, _ = x.shape
    rows_lat = seq // GRID_W
    xa = jnp.concatenate([ctx, x], axis=1)
    cvec = jnp.zeros((MOD_ROWS, D_MODEL), F32).at[0:b].set(c).at[MOD_ROWS - 1].set(c_ctx)
    mod_all = _modulation(cvec, w_mod, b_mod)
    win_base, win_cfg_id, win_cfgs = _attn_window_plan(rows_lat)
    cos_t, sin_t = _rope_tables(seq)
    sel = _head_select_matrix()
    w_in_p = jnp.pad(w_in, ((0, 0), (0, 0), (0, D_IN_PAD - D_IN_PROJ))).astype(BF16)
    w_out16 = w_out.astype(BF16)

    for layer in range(DEPTH):
        last = layer == DEPTH - 1
        is_moe = layer % 2 == 1
        mod = mod_all[layer].reshape(MOD_ROWS, 1, 6 * D_MODEL)
        qkv, z, xbc, dt = _in_proj(xa, mod, w_in_p[layer], tm=TM_ROW)
        a = _attention(qkv, _expand_rel_bias(na_rpb[layer], win_cfgs), jnp.asarray(win_base),
                       jnp.asarray(win_cfg_id), with_ctx_out=not last)
        y = _ssd_scan(xbc, dt, conv_w[layer], conv_b[layer], dt_bias[layer], a_log[layer], d_skip[layer],
                      cos_t, sin_t, sel)
        router_p = None
        if is_moe:
            r_pad = jnp.pad(router_w[layer // 2], ((0, 0), (0, LANES - N_EXPERTS)))
            r_hi = r_pad.astype(BF16)
            r_mid = (r_pad - r_hi.astype(F32)).astype(BF16)
            router_p = jnp.concatenate([r_hi, r_mid, r_hi], axis=0)
        row0, tm_post = (CTX_LEN, TM_LAST) if last else (0, TM_ROW)
        outs = _out_proj(a, y, z, xa, mod, w_out16[layer], ssd_norm_w[layer], ln_g[layer, 0], ln_b[layer, 0],
                         router_p, tm=tm_post, h2_dtype=F32 if is_moe else BF16, row0=row0)
        if is_moe:
            x1, h2, eidx, gates = outs
            m = layer // 2
            xa = _moe_ffn(h2, eidx, gates, x1, mod, moe_w_gate.reshape(-1, *moe_w_gate.shape[2:]),
                          moe_w_up.reshape(-1, *moe_w_up.shape[2:]), moe_w_down.reshape(-1, *moe_w_down.shape[2:]),
                          ln_g[layer, 1], ln_b[layer, 1], first_expert=m * N_EXPERTS, row0=row0,
                          tm_row=tm_post, tm_expert=TM_EXPERT, tf=TF_EXPERT)
        else:
            assert not last
            x1, h2 = outs
            m = layer // 2
            xa = _dense_ffn(h2, x1, mod, ffn_w_gate[m].astype(BF16), ffn_w_up[m].astype(BF16),
                            ffn_w_down[m].astype(BF16), ln_g[layer, 1], ln_b[layer, 1], tm=TM_ROW, tf=TF_DENSE)
    return xa
```

```python
import functools
import math

import numpy as np
import jax
import jax.numpy as jnp
from jax import lax
from jax.experimental import pallas as pl
from jax.experimental.pallas import tpu as pltpu
from jax.experimental.pallas import tpu_sc as plsc

F32 = jnp.float32
BF16 = jnp.bfloat16
HIGHEST = lax.Precision.HIGHEST

D_MODEL = 1024
DEPTH = 4
GRID_W = 64
CTX_LEN = 256
N_HEADS_NA = 8
HEAD_DIM_NA = 64
D_NA = N_HEADS_NA * HEAD_DIM_NA
WIN_R = 8
WIN_C = 16
N_HEADS_SSD = 8
HEAD_DIM_SSD = 64
D_SSD = N_HEADS_SSD * HEAD_DIM_SSD
N_GROUPS_SSD = 2
D_STATE = 128
D_CONV = 5
CHUNK = 128
ROPE_BASE = 10000.0
GN = N_GROUPS_SSD * D_STATE
CONV_CH = D_SSD + 2 * GN
D_IN_PROJ = 3 * D_NA + D_SSD + CONV_CH + 2 * N_HEADS_SSD
N_EXPERTS = 8
TOP_K = 2
DEEPNORM_ALPHA = (2.0 * DEPTH) ** 0.25
LN_EPS = 1e-5
RMS_EPS = 1e-5

LANES = 128
SUBLANES = 8
VMEM_LIMIT = 56 * 1024 * 1024
DT_PAD = LANES
D_IN_PAD = D_IN_PROJ - 2 * N_HEADS_SSD + DT_PAD
OFF_Z = 3 * D_NA
OFF_XBC = OFF_Z + D_SSD
OFF_DT = OFF_XBC + CONV_CH
MOD_ROWS = 24
HEAD_PAIR = 2 * HEAD_DIM_NA
ROWS_PER_STEP = 2
WIN_ROWS = WIN_R + ROWS_PER_STEP - 1


def _cparams(*sem):
    return pltpu.CompilerParams(dimension_semantics=sem, vmem_limit_bytes=VMEM_LIMIT)


def _silu(v):
    return v * jax.nn.sigmoid(v)


def _layernorm_rows(v, g, b):
    mu = jnp.mean(v, axis=-1, keepdims=True)
    d = v - mu
    var = jnp.mean(d * d, axis=-1, keepdims=True)
    return d * lax.rsqrt(var + LN_EPS) * g + b


def _pack_bf16_pairs(v):
    n = v.shape[1] // 2
    lo = pltpu.bitcast(v[:, 0:n].astype(BF16).astype(F32), jnp.uint32)
    hi = pltpu.bitcast(v[:, n:2 * n].astype(BF16).astype(F32), jnp.uint32)
    return hi | (lo >> 16)


def _unpack_bf16_pairs(w):
    lo = pltpu.bitcast(w << 16, F32)
    hi = pltpu.bitcast(w & jnp.uint32(0xFFFF0000), F32)
    return jnp.concatenate([lo, hi], axis=1)


def _row_is_ctx(tm, j, row0=0):
    row = lax.broadcasted_iota(jnp.int32, (tm, 1), 0) + j * tm + row0
    return row < CTX_LEN


def _mod_pick(is_ctx, modc_ref, mod_ref, seg):
    lo, hi = seg * D_MODEL, (seg + 1) * D_MODEL
    return jnp.where(is_ctx, modc_ref[0, :, lo:hi], mod_ref[0, :, lo:hi])


def _mod_kernel(c_ref, w_ref, b_ref, o_ref):
    cs = _silu(c_ref[...])
    o_ref[0] = jnp.dot(cs, w_ref[0], preferred_element_type=F32, precision=HIGHEST) + b_ref[0]


def _modulation(cvec, w_mod, b_mod):
    tn = D_MODEL
    return pl.pallas_call(
        _mod_kernel,
        grid=(DEPTH, 6 * D_MODEL // tn),
        in_specs=[
            pl.BlockSpec((MOD_ROWS, D_MODEL), lambda l, n: (0, 0)),
            pl.BlockSpec((1, D_MODEL, tn), lambda l, n: (l, 0, n)),
            pl.BlockSpec((1, 1, tn), lambda l, n: (l, 0, n)),
        ],
        out_specs=pl.BlockSpec((1, MOD_ROWS, tn), lambda l, n: (l, 0, n)),
        out_shape=jax.ShapeDtypeStruct((DEPTH, MOD_ROWS, 6 * D_MODEL), F32),
        compiler_params=_cparams("arbitrary", "arbitrary"),
        name="modulation",
    )(cvec, w_mod, b_mod.reshape(DEPTH, 1, 6 * D_MODEL))


def _in_proj_kernel(x_ref, mod_ref, modc_ref, w_ref, qkv_ref, z_ref, xbc_ref, dt_ref, *, tm):
    is_ctx = _row_is_ctx(tm, pl.program_id(1))
    sh = _mod_pick(is_ctx, modc_ref, mod_ref, 0)
    sc = _mod_pick(is_ctx, modc_ref, mod_ref, 1)
    h = (x_ref[0] * (1.0 + sc) + sh).astype(BF16)
    qkv_ref[0] = jnp.dot(h, w_ref[:, 0:OFF_Z], preferred_element_type=F32).astype(BF16)
    z_ref[0] = jnp.dot(h, w_ref[:, OFF_Z:OFF_XBC], preferred_element_type=F32)
    xbc_ref[0] = jnp.dot(h, w_ref[:, OFF_XBC:OFF_DT], preferred_element_type=F32)
    dt_ref[0] = jnp.dot(h, w_ref[:, OFF_DT:D_IN_PAD], preferred_element_type=F32)


def _in_proj(xa, mod, w_in_p, *, tm):
    b, rows, _ = xa.shape
    row_spec = lambda n: pl.BlockSpec((1, tm, n), lambda i, j: (i, j, 0))
    return pl.pallas_call(
        functools.partial(_in_proj_kernel, tm=tm),
        grid=(b, rows // tm),
        in_specs=[
            row_spec(D_MODEL),
            pl.BlockSpec((1, 1, 6 * D_MODEL), lambda i, j: (i, 0, 0)),
            pl.BlockSpec((1, 1, 6 * D_MODEL), lambda i, j: (MOD_ROWS - 1, 0, 0)),
            pl.BlockSpec((D_MODEL, D_IN_PAD), lambda i, j: (0, 0)),
        ],
        out_specs=[row_spec(OFF_Z), row_spec(D_SSD), row_spec(CONV_CH), row_spec(DT_PAD)],
        out_shape=[
            jax.ShapeDtypeStruct((b, rows, OFF_Z), BF16),
            jax.ShapeDtypeStruct((b, rows, D_SSD), F32),
            jax.ShapeDtypeStruct((b, rows, CONV_CH), F32),
            jax.ShapeDtypeStruct((b, rows, DT_PAD), F32),
        ],
        compiler_params=_cparams("arbitrary", "arbitrary"),
        name="in_proj",
    )(xa, mod, mod, w_in_p)


def _attn_head_pair(q_pair, keys_vals, biases):
    m_rows = q_pair.shape[0]
    lane = lax.broadcasted_iota(jnp.int32, q_pair.shape, 1)
    zero = jnp.zeros_like(q_pair)
    q_stack = jnp.concatenate([jnp.where(lane < HEAD_DIM_NA, q_pair, zero),
                               jnp.where(lane >= HEAD_DIM_NA, q_pair, zero)], axis=0)
    scores = []
    for (k, _), bias in zip(keys_vals, biases):
        s = lax.dot_general(q_stack, k, (((1,), (1,)), ((), ())), preferred_element_type=F32)
        scores.append(s if bias is None else s + bias)
    m = scores[0].max(axis=-1, keepdims=True)
    for s in scores[1:]:
        m = jnp.maximum(m, s.max(axis=-1, keepdims=True))
    denom = jnp.zeros_like(m)
    acc = jnp.zeros((2 * m_rows, HEAD_PAIR), F32)
    for s, (_, v) in zip(scores, keys_vals):
        p = jnp.exp(s - m)
        denom = denom + p.sum(axis=-1, keepdims=True)
        acc = acc + jnp.dot(p.astype(BF16), v, preferred_element_type=F32)
    out = acc / denom
    return jnp.where(lane < HEAD_DIM_NA, out[0:m_rows], out[m_rows:2 * m_rows])


def _attn_kernel(base_ref, cfg_ref, qkv_ref, bias_ref, o_ref, *, rows_lat, with_ctx_out):
    scale = HEAD_DIM_NA ** -0.5
    k_off, v_off = D_NA, 2 * D_NA
    n_pairs = N_HEADS_NA // 2
    q_rows = ROWS_PER_STEP * GRID_W
    win = WIN_ROWS * GRID_W

    def lat_rows(j, carry):
        q_base = pl.multiple_of(CTX_LEN + j * q_rows, q_rows)
        k_base = pl.multiple_of(CTX_LEN + base_ref[j] * GRID_W, GRID_W)
        cfg = cfg_ref[j]
        for p in range(n_pairs):
            lo = p * HEAD_PAIR
            q = qkv_ref[0, pl.ds(q_base, q_rows), lo:lo + HEAD_PAIR] * scale
            k_loc = qkv_ref[0, pl.ds(k_base, win), k_off + lo:k_off + lo + HEAD_PAIR]
            v_loc = qkv_ref[0, pl.ds(k_base, win), v_off + lo:v_off + lo + HEAD_PAIR]
            k_ctx = qkv_ref[0, 0:CTX_LEN, k_off + lo:k_off + lo + HEAD_PAIR]
            v_ctx = qkv_ref[0, 0:CTX_LEN, v_off + lo:v_off + lo + HEAD_PAIR]
            o = _attn_head_pair(q, [(k_loc, v_loc), (k_ctx, v_ctx)], [bias_ref[cfg, p], None])
            o_ref[0, pl.ds(q_base, q_rows), lo:lo + HEAD_PAIR] = o.astype(o_ref.dtype)
        return carry

    lax.fori_loop(0, rows_lat // ROWS_PER_STEP, lat_rows, 0)

    if with_ctx_out:
        for p in range(n_pairs):
            lo = p * HEAD_PAIR
            q = qkv_ref[0, 0:CTX_LEN, lo:lo + HEAD_PAIR] * scale
            k_ctx = qkv_ref[0, 0:CTX_LEN, k_off + lo:k_off + lo + HEAD_PAIR]
            v_ctx = qkv_ref[0, 0:CTX_LEN, v_off + lo:v_off + lo + HEAD_PAIR]
            o = _attn_head_pair(q, [(k_ctx, v_ctx)], [None])
            o_ref[0, 0:CTX_LEN, lo:lo + HEAD_PAIR] = o.astype(o_ref.dtype)
    else:
        o_ref[0, 0:CTX_LEN, :] = jnp.zeros((CTX_LEN, D_NA), o_ref.dtype)


def _attention(qkv, bias, win_base, cfg, *, with_ctx_out):
    b, rows, _ = qkv.shape
    rows_lat = (rows - CTX_LEN) // GRID_W
    grid_spec = pltpu.PrefetchScalarGridSpec(
        num_scalar_prefetch=2,
        grid=(b,),
        in_specs=[
            pl.BlockSpec((1, rows, OFF_Z), lambda i, wb, cf: (i, 0, 0)),
            pl.BlockSpec(bias.shape, lambda i, wb, cf: (0, 0, 0, 0)),
        ],
        out_specs=pl.BlockSpec((1, rows, D_NA), lambda i, wb, cf: (i, 0, 0)),
    )
    return pl.pallas_call(
        functools.partial(_attn_kernel, rows_lat=rows_lat, with_ctx_out=with_ctx_out),
        grid_spec=grid_spec,
        out_shape=jax.ShapeDtypeStruct((b, rows, D_NA), BF16),
        compiler_params=_cparams("arbitrary"),
        name="attention",
    )(win_base, cfg, qkv, bias)


def _attn_window_plan(rows_lat):
    kr = min(WIN_R, rows_lat)
    assert rows_lat % ROWS_PER_STEP == 0 and rows_lat >= WIN_ROWS and kr == WIN_R
    bases, cfg_ids, configs = [], [], []
    for i in range(0, rows_lat, ROWS_PER_STEP):
        r0 = [int(np.clip(i + s - kr // 2, 0, rows_lat - kr)) for s in range(ROWS_PER_STEP)]
        base = min(r0[0], rows_lat - WIN_ROWS)
        assert all(0 <= r - base and r - base + kr <= WIN_ROWS for r in r0)
        c = tuple((i + s - r0[s], r0[s] - base) for s in range(ROWS_PER_STEP))
        if c not in configs:
            configs.append(c)
        bases.append(base)
        cfg_ids.append(configs.index(c))
    return np.asarray(bases, np.int32), np.asarray(cfg_ids, np.int32), configs


def _expand_rel_bias(rpb, configs):
    cols = np.arange(GRID_W)
    col_start = np.clip(cols - WIN_C // 2, 0, GRID_W - WIN_C)
    kc = cols[None, :]
    in_win = (kc >= col_start[:, None]) & (kc < col_start[:, None] + WIN_C)
    col_idx = np.clip(kc - cols[:, None] + WIN_C - 1, 0, 2 * WIN_C - 2)
    d = np.asarray([[c[s][0] for s in range(ROWS_PER_STEP)] for c in configs])[:, :, None]
    o = np.asarray([[c[s][1] for s in range(ROWS_PER_STEP)] for c in configs])[:, :, None]
    r = np.arange(WIN_ROWS)[None, None, :] - o
    row_ok = (r >= 0) & (r < WIN_R)
    row_idx = np.clip(r - d + WIN_R - 1, 0, 2 * WIN_R - 2)
    t = rpb[:, row_idx]
    t = t[..., col_idx]
    ok = row_ok[None, :, :, :, None, None] & in_win[None, None, None, None]
    t = jnp.where(jnp.asarray(ok), t.astype(F32), -jnp.inf)
    t = t.transpose(1, 0, 2, 4, 3, 5)
    return t.reshape(len(configs), N_HEADS_NA // 2, 2 * ROWS_PER_STEP * GRID_W, WIN_ROWS * GRID_W)


def _split3(v):
    hi = v.astype(BF16)
    rest = v - hi.astype(F32)
    mid = rest.astype(BF16)
    lo = (rest - mid.astype(F32)).astype(BF16)
    return hi, mid, lo


def _ssd_kernel(xbc_ref, dt_ref, convw_ref, convb_ref, dtb_ref, alog_ref, dskip_ref, cos_ref, sin_ref,
                sel_ref, y_ref, htf_ref, htb_ref, stf_ref, c_ref, ef_ref, ext_ref, *, n_chunks):
    n_ctx_chunks = CTX_LEN // CHUNK
    hp = D_SSD // N_GROUPS_SSD
    htf_ref[...] = jnp.zeros_like(htf_ref)
    htb_ref[...] = jnp.zeros_like(htb_ref)

    li = lax.broadcasted_iota(jnp.int32, (CHUNK, CHUNK), 0)
    si = lax.broadcasted_iota(jnp.int32, (CHUNK, CHUNK), 1)
    lower = li >= si
    upper = li <= si
    tri = jnp.concatenate([lower, upper], axis=0).astype(F32).astype(BF16)
    lane_c = lax.broadcasted_iota(jnp.int32, (CHUNK, LANES), 1)
    rope_lo = (lane_c % (D_STATE // 2)) < (D_STATE // 4)
    pair_lane = lax.broadcasted_iota(jnp.int32, (CHUNK, HEAD_PAIR), 1)

    def rope(w, cos, sin):
        partner = jnp.where(rope_lo, pltpu.roll(w, LANES - D_STATE // 4, 1), pltpu.roll(w, D_STATE // 4, 1))
        return w * cos + partner * sin

    def local_pass(k, carry):
        c = jnp.where(k < n_ctx_chunks, n_ctx_chunks - 1 - k, n_chunks + n_ctx_chunks - 1 - k)
        base = pl.multiple_of(c * CHUNK, CHUNK)
        is_start = (c == 0) | (c == n_ctx_chunks)
        is_end = (c == n_ctx_chunks - 1) | (c == n_chunks - 1)
        prev_base = pl.multiple_of(jnp.maximum(base - SUBLANES, 0), SUBLANES)
        next_base = pl.multiple_of(jnp.minimum(base + CHUNK, (n_chunks - 1) * CHUNK + CHUNK - SUBLANES), SUBLANES)
        ext_ref[0:SUBLANES, :] = jnp.where(is_start, 0.0, xbc_ref[0, pl.ds(prev_base, SUBLANES), :])
        ext_ref[SUBLANES:SUBLANES + CHUNK, :] = xbc_ref[0, pl.ds(base, CHUNK), :]
        ext_ref[SUBLANES + CHUNK:2 * SUBLANES + CHUNK, :] = jnp.where(
            is_end, 0.0, xbc_ref[0, pl.ds(next_base, SUBLANES), :])
        ext = ext_ref[...]
        conv = convb_ref[...]
        for t in range(D_CONV):
            shift = (D_CONV // 2 - t) % ext.shape[0]
            tap = ext if shift == 0 else pltpu.roll(ext, shift, 0)
            conv = conv + tap[SUBLANES:SUBLANES + CHUNK] * convw_ref[t:t + 1, :]
        u = _silu(conv)
        xs = u[:, 0:D_SSD]
        cos = cos_ref[pl.ds(base, CHUNK), :]
        sin = sin_ref[pl.ds(base, CHUNK), :]
        bm = [rope(u[:, D_SSD + g * D_STATE:D_SSD + (g + 1) * D_STATE], cos, sin) for g in range(N_GROUPS_SSD)]
        cm = [rope(u[:, D_SSD + GN + g * D_STATE:D_SSD + GN + (g + 1) * D_STATE], cos, sin)
              for g in range(N_GROUPS_SSD)]

        dt = jax.nn.softplus(dt_ref[0, pl.ds(base, CHUNK), :] + dtb_ref[...])
        a = dt * (-jnp.exp(alog_ref[...]))
        sums = jnp.dot(tri, jnp.concatenate(_split3(a), axis=1), preferred_element_type=F32)
        sums = sums[:, 0:LANES] + sums[:, LANES:2 * LANES] + sums[:, 2 * LANES:3 * LANES]
        acs = jnp.where(lane_c < N_HEADS_SSD, sums[0:CHUNK], sums[CHUNK:2 * CHUNK])
        acs_t = acs.T
        dt_t = dt.T
        wide = jnp.dot(jnp.concatenate(_split3(jnp.concatenate([acs, dt], axis=0)), axis=1), sel_ref[...],
                       preferred_element_type=F32)
        acs_x, dt_x = wide[0:CHUNK], wide[CHUNK:2 * CHUNK]
        arg_f, arg_b = acs_x[:, 0:D_SSD], acs_x[:, D_SSD:2 * D_SSD]
        e_f = jnp.exp(arg_f)
        e_b = jnp.exp(arg_b)
        x_f = (xs * dt_x[:, 0:D_SSD] * jnp.exp(arg_f[CHUNK - 1:CHUNK, :] - arg_f)).astype(BF16)
        x_b = (xs * dt_x[:, D_SSD:2 * D_SSD] * jnp.exp(arg_b[0:1, :] - arg_b)).astype(BF16)

        xs16 = xs.astype(BF16)
        y_parts = []
        for g in range(N_GROUPS_SSD):
            bt = bm[g].T.astype(BF16)
            c16 = cm[g].astype(BF16)
            gmat = jnp.dot(c16, bt, preferred_element_type=F32)
            gs = slice(g * hp, (g + 1) * hp)
            heads = range(g * (N_HEADS_SSD // N_GROUPS_SSD), (g + 1) * (N_HEADS_SSD // N_GROUPS_SSD))
            ms = []
            for h in heads:
                jf, jb = h, N_HEADS_SSD + h
                wf = jnp.where(lower, jnp.exp(acs[:, jf:jf + 1] - acs_t[jf:jf + 1, :]), 0.0) * dt_t[jf:jf + 1, :]
                wb = jnp.where(upper, jnp.exp(acs[:, jb:jb + 1] - acs_t[jb:jb + 1, :]), 0.0) * dt_t[jb:jb + 1, :]
                ms.append((gmat * (wf + wb)).astype(BF16))
            y_g = []
            for q in range(len(ms) // 2):
                lo = g * hp + q * HEAD_PAIR
                xp = xs16[:, lo:lo + HEAD_PAIR]
                rhs = jnp.concatenate([jnp.where(pair_lane < HEAD_DIM_SSD, xp, jnp.zeros_like(xp)),
                                       jnp.where(pair_lane >= HEAD_DIM_SSD, xp, jnp.zeros_like(xp))], axis=0)
                lhs = jnp.concatenate([ms[2 * q], ms[2 * q + 1]], axis=1)
                y_g.append(jnp.dot(lhs, rhs, preferred_element_type=F32))
            y_diag = jnp.concatenate(y_g, axis=1)
            y_off = jnp.dot(c16, htb_ref[g].astype(BF16), preferred_element_type=F32) * e_b[:, gs]
            y_parts.append(y_diag + y_off)
            htb_ref[g] = htb_ref[g] * e_b[0:1, gs] + jnp.dot(bt, x_b[:, gs], preferred_element_type=F32)
            stf_ref[c, g] = jnp.dot(bt, x_f[:, gs], preferred_element_type=F32)
            c_ref[c, g] = c16
        ef_ref[c] = e_f
        y_ref[0, pl.ds(base, CHUNK), :] = jnp.concatenate(y_parts, axis=1) + dskip_ref[...] * xs
        return carry

    lax.fori_loop(0, n_chunks, local_pass, 0)

    def forward_pass(c, carry):
        base = pl.multiple_of(c * CHUNK, CHUNK)
        e_f = ef_ref[c]
        y_parts = []
        for g in range(N_GROUPS_SSD):
            gs = slice(g * hp, (g + 1) * hp)
            y_parts.append(jnp.dot(c_ref[c, g], htf_ref[g].astype(BF16), preferred_element_type=F32) * e_f[:, gs])
            htf_ref[g] = htf_ref[g] * e_f[CHUNK - 1:CHUNK, gs] + stf_ref[c, g]
        y_ref[0, pl.ds(base, CHUNK), :] = y_ref[0, pl.ds(base, CHUNK), :] + jnp.concatenate(y_parts, axis=1)
        return carry

    lax.fori_loop(0, n_chunks, forward_pass, 0)


def _ssd_scan(xbc, dt, conv_w, conv_b, dt_bias, a_log, d_skip, cos_t, sin_t, sel):
    b, rows, _ = xbc.shape
    n_chunks = rows // CHUNK
    hp = D_SSD // N_GROUPS_SSD
    full = lambda a: pl.BlockSpec(a.shape, lambda i: (0,) * a.ndim)
    pad16 = lambda v: jnp.pad(v.reshape(1, -1), ((0, 0), (0, DT_PAD - v.size)))
    conv_w8 = jnp.pad(conv_w, ((0, SUBLANES - D_CONV), (0, 0)))
    params = [conv_w8, conv_b.reshape(1, -1), pad16(dt_bias), pad16(a_log),
              jnp.repeat(d_skip, HEAD_DIM_SSD).reshape(1, -1), cos_t, sin_t, sel]
    return pl.pallas_call(
        functools.partial(_ssd_kernel, n_chunks=n_chunks),
        grid=(b,),
        in_specs=[pl.BlockSpec((1, rows, CONV_CH), lambda i: (i, 0, 0)),
                  pl.BlockSpec((1, rows, DT_PAD), lambda i: (i, 0, 0))] + [full(p) for p in params],
        out_specs=pl.BlockSpec((1, rows, D_SSD), lambda i: (i, 0, 0)),
        out_shape=jax.ShapeDtypeStruct((b, rows, D_SSD), F32),
        scratch_shapes=[
            pltpu.VMEM((N_GROUPS_SSD, D_STATE, hp), F32),
            pltpu.VMEM((N_GROUPS_SSD, D_STATE, hp), F32),
            pltpu.VMEM((n_chunks, N_GROUPS_SSD, D_STATE, hp), F32),
            pltpu.VMEM((n_chunks, N_GROUPS_SSD, CHUNK, D_STATE), BF16),
            pltpu.VMEM((n_chunks, CHUNK, D_SSD), F32),
            pltpu.VMEM((CHUNK + 2 * SUBLANES, CONV_CH), F32),
        ],
        compiler_params=_cparams("arbitrary"),
        name="ssd_scan",
    )(xbc, dt, *params)


def _rope_tables(seq):
    half, quarter = D_STATE // 2, D_STATE // 4
    t = jnp.arange(seq)
    inv_freq = ROPE_BASE ** (-jnp.arange(quarter, dtype=F32) / quarter)
    ang_r = (t // GRID_W).astype(F32)[:, None] * inv_freq
    ang_c = (t % GRID_W).astype(F32)[:, None] * inv_freq
    cos = jnp.concatenate([jnp.cos(ang_r)] * 2 + [jnp.cos(ang_c)] * 2, axis=-1)
    sin = jnp.concatenate([-jnp.sin(ang_r), jnp.sin(ang_r), -jnp.sin(ang_c), jnp.sin(ang_c)], axis=-1)
    cos = jnp.concatenate([jnp.ones((CTX_LEN, 2 * half), F32), cos], axis=0)
    sin = jnp.concatenate([jnp.zeros((CTX_LEN, 2 * half), F32), sin], axis=0)
    return cos, sin


def _head_select_matrix():
    sel = np.zeros((DT_PAD, 2 * D_SSD), np.float32)
    for j in range(2 * N_HEADS_SSD):
        sel[j, j * HEAD_DIM_SSD:(j + 1) * HEAD_DIM_SSD] = 1.0
    return jnp.asarray(np.concatenate([sel] * 3, axis=0), dtype=BF16)


def _out_proj_kernel(*refs, tm, with_router, row0):
    if with_router:
        (a_ref, y_ref, z_ref, x_ref, mod_ref, modc_ref, w_ref, nw_ref, g_ref, b_ref, r_ref,
         x1_ref, h2_ref, ei_ref, gt_ref) = refs
    else:
        a_ref, y_ref, z_ref, x_ref, mod_ref, modc_ref, w_ref, nw_ref, g_ref, b_ref, x1_ref, h2_ref = refs
    is_ctx = _row_is_ctx(tm, pl.program_id(1), row0)
    yg = y_ref[0] * _silu(z_ref[0])
    gw = D_SSD // N_GROUPS_SSD
    parts = []
    for g in range(N_GROUPS_SSD):
        part = yg[:, g * gw:(g + 1) * gw]
        parts.append(part * lax.rsqrt(jnp.mean(part * part, axis=-1, keepdims=True) + RMS_EPS))
    s = (jnp.concatenate(parts, axis=1) * nw_ref[...]).astype(BF16)
    mix = (jnp.dot(a_ref[0], w_ref[0:D_NA, :], preferred_element_type=F32)
           + jnp.dot(s, w_ref[D_NA:D_NA + D_SSD, :], preferred_element_type=F32))
    g1 = _mod_pick(is_ctx, modc_ref, mod_ref, 2)
    x1 = _layernorm_rows(DEEPNORM_ALPHA * x_ref[0] + (1.0 + g1) * mix, g_ref[...], b_ref[...])
    x1_ref[0] = x1
    sh2 = _mod_pick(is_ctx, modc_ref, mod_ref, 3)
    sc2 = _mod_pick(is_ctx, modc_ref, mod_ref, 4)
    h2 = x1 * (1.0 + sc2) + sh2
    if with_router:
        h2_ref[0] = _pack_bf16_pairs(h2)
    else:
        h2_ref[0] = h2.astype(h2_ref.dtype)
    if with_router:
        h_hi = h2.astype(BF16)
        h_mid = (h2 - h_hi.astype(F32)).astype(BF16)
        logits = jnp.dot(jnp.concatenate([h_hi, h_hi, h_mid], axis=1), r_ref[...],
                         preferred_element_type=F32)
        lane = lax.broadcasted_iota(jnp.int32, logits.shape, 1)
        logits = jnp.where(lane < N_EXPERTS, logits, -jnp.inf)
        m1 = logits.max(axis=-1, keepdims=True)
        i1 = jnp.where(logits == m1, lane, LANES).min(axis=-1, keepdims=True)
        rest = jnp.where(lane == i1, -jnp.inf, logits)
        m2 = rest.max(axis=-1, keepdims=True)
        i2 = jnp.where(rest == m2, lane, LANES).min(axis=-1, keepdims=True)
        e2 = jnp.exp(m2 - m1)
        gate1 = 1.0 / (1.0 + e2)
        gate2 = e2 / (1.0 + e2)
        ei_ref[0] = jnp.where(lane == 0, i1, jnp.where(lane == 1, i2, 0))
        gt_ref[0] = jnp.where(lane == 0, gate1, jnp.where(lane == 1, gate2, 0.0))


def _out_proj(a, y, z, xa, mod, w_out, norm_w, ln_g, ln_b, router_p, *, tm, row0=0):
    b, rows_in, _ = xa.shape
    rows = rows_in - row0
    assert row0 % tm == 0 and rows % tm == 0
    with_router = router_p is not None
    row_spec = lambda n: pl.BlockSpec((1, tm, n), lambda i, j: (i, j, 0))
    in_row_spec = lambda n: pl.BlockSpec((1, tm, n), lambda i, j: (i, j + row0 // tm, 0))
    vec = lambda n: pl.BlockSpec((1, n), lambda i, j: (0, 0))
    in_specs = [
        in_row_spec(D_NA), in_row_spec(D_SSD), in_row_spec(D_SSD), in_row_spec(D_MODEL),
        pl.BlockSpec((1, 1, 6 * D_MODEL), lambda i, j: (i, 0, 0)),
        pl.BlockSpec((1, 1, 6 * D_MODEL), lambda i, j: (MOD_ROWS - 1, 0, 0)),
        pl.BlockSpec((D_NA + D_SSD, D_MODEL), lambda i, j: (0, 0)),
        vec(D_SSD), vec(D_MODEL), vec(D_MODEL),
    ]
    args = [a, y, z, xa, mod, mod, w_out, norm_w.reshape(1, -1), ln_g.reshape(1, -1), ln_b.reshape(1, -1)]
    h2_cols, h2_dtype = (D_MODEL // 2, jnp.uint32) if with_router else (D_MODEL, BF16)
    out_specs = [row_spec(D_MODEL), row_spec(h2_cols)]
    out_shape = [jax.ShapeDtypeStruct((b, rows, D_MODEL), F32), jax.ShapeDtypeStruct((b, rows, h2_cols), h2_dtype)]
    if with_router:
        in_specs.append(pl.BlockSpec((3 * D_MODEL, LANES), lambda i, j: (0, 0)))
        args.append(router_p)
        out_specs += [row_spec(LANES), row_spec(LANES)]
        out_shape += [jax.ShapeDtypeStruct((b, rows, LANES), jnp.int32), jax.ShapeDtypeStruct((b, rows, LANES), F32)]
    return pl.pallas_call(
        functools.partial(_out_proj_kernel, tm=tm, with_router=with_router, row0=row0),
        grid=(b, rows // tm),
        in_specs=in_specs,
        out_specs=out_specs,
        out_shape=out_shape,
        compiler_params=_cparams("arbitrary", "arbitrary"),
        name="out_proj",
    )(*args)


def _ffn_kernel(h_ref, x_ref, mod_ref, modc_ref, wg_ref, wu_ref, wd_ref, g_ref, b_ref, o_ref, *, tm, tf):
    is_ctx = _row_is_ctx(tm, pl.program_id(1))
    h = h_ref[0]
    d_ff = wg_ref.shape[1]
    acc = jnp.zeros((tm, D_MODEL), F32)
    for f in range(d_ff // tf):
        fs = slice(f * tf, (f + 1) * tf)
        gate = jnp.dot(h, wg_ref[:, fs], preferred_element_type=F32)
        up = jnp.dot(h, wu_ref[:, fs], preferred_element_type=F32)
        acc = acc + jnp.dot((_silu(gate) * up).astype(BF16), wd_ref[fs, :], preferred_element_type=F32)
    g2 = _mod_pick(is_ctx, modc_ref, mod_ref, 5)
    o_ref[0] = _layernorm_rows(DEEPNORM_ALPHA * x_ref[0] + (1.0 + g2) * acc, g_ref[...], b_ref[...])


def _dense_ffn(h2, x1, mod, wg, wu, wd, ln_g, ln_b, *, tm, tf):
    b, rows, _ = x1.shape
    d_ff = wg.shape[1]
    row_spec = lambda n: pl.BlockSpec((1, tm, n), lambda i, j: (i, j, 0))
    vec = lambda n: pl.BlockSpec((1, n), lambda i, j: (0, 0))
    const = lambda s: pl.BlockSpec(s, lambda i, j: (0, 0), pipeline_mode=pl.Buffered(1))
    return pl.pallas_call(
        functools.partial(_ffn_kernel, tm=tm, tf=tf),
        grid=(b, rows // tm),
        in_specs=[
            row_spec(D_MODEL), row_spec(D_MODEL),
            pl.BlockSpec((1, 1, 6 * D_MODEL), lambda i, j: (i, 0, 0)),
            pl.BlockSpec((1, 1, 6 * D_MODEL), lambda i, j: (MOD_ROWS - 1, 0, 0)),
            const((D_MODEL, d_ff)), const((D_MODEL, d_ff)), const((d_ff, D_MODEL)),
            vec(D_MODEL), vec(D_MODEL),
        ],
        out_specs=row_spec(D_MODEL),
        out_shape=jax.ShapeDtypeStruct((b, rows, D_MODEL), F32),
        compiler_params=_cparams("arbitrary", "arbitrary"),
        name="dense_ffn",
    )(h2, x1, mod, mod, wg, wu, wd, ln_g.reshape(1, -1), ln_b.reshape(1, -1))


def _gather_rows(table, idx):
    n, d = idx.shape[0], table.shape[1]
    workers = SC_CORES * SC_SUBCORES
    n_chunks = n // (workers * GATHER_CHUNK)
    assert n == workers * n_chunks * GATHER_CHUNK
    per_worker = n_chunks * GATHER_CHUNK
    mesh = plsc.VectorSubcoreMesh(core_axis_name="c", subcore_axis_name="s")

    assert n_chunks % 2 == 0

    def body(table_hbm, idx_hbm, out_hbm, idx_v, rows_v, sems):
        wid = lax.axis_index("s") * SC_CORES + lax.axis_index("c")
        pltpu.sync_copy(idx_hbm.at[wid], idx_v)
        base = wid * per_worker

        def gather(j, slot):
            return pltpu.make_async_copy(table_hbm.at[idx_v.at[j]], rows_v.at[slot], sems.at[slot])

        gather(0, 0).start()

        @pl.loop(0, n_chunks, step=2)
        def _(j0):
            for slot in range(2):
                j = j0 + slot
                gather(j, slot).wait()

                @pl.when(j + 1 < n_chunks)
                def _():
                    gather(j + 1, 1 - slot).start()

                pltpu.sync_copy(rows_v.at[slot], out_hbm.at[pl.ds(base + j * GATHER_CHUNK, GATHER_CHUNK)])

    return pl.kernel(
        body,
        out_type=jax.ShapeDtypeStruct((n, d), table.dtype),
        mesh=mesh,
        scratch_types=[
            pltpu.VMEM((n_chunks, GATHER_CHUNK), jnp.int32),
            pltpu.VMEM((2, GATHER_CHUNK, d), table.dtype),
            pltpu.SemaphoreType.DMA((2,)),
        ],
        name="gather_rows",
    )(table, idx.reshape(workers, n_chunks, GATHER_CHUNK))


def _expert_kernel(te_ref, nu_ref, x_ref, wg_ref, wu_ref, wd_ref, y_ref, x16_ref, acc_ref):
    i, f = pl.program_id(0), pl.program_id(1)
    live = i < nu_ref[0]

    @pl.when(f == 0)
    def _():
        acc_ref[...] = jnp.zeros_like(acc_ref)
        x16_ref[...] = _unpack_bf16_pairs(x_ref[...]).astype(BF16)

    @pl.when(live)
    def _():
        h = x16_ref[...]
        gate = jnp.dot(h, wg_ref[0].astype(BF16), preferred_element_type=F32)
        up = jnp.dot(h, wu_ref[0].astype(BF16), preferred_element_type=F32)
        acc_ref[...] += jnp.dot((_silu(gate) * up).astype(BF16), wd_ref[0].astype(BF16),
                                preferred_element_type=F32)

    @pl.when(f == pl.num_programs(1) - 1)
    def _():
        y_ref[...] = _pack_bf16_pairs(acc_ref[...])


def _expert_ffn(xs, tile_expert, n_used, wg, wu, wd, *, tm, tf):
    n_rows = xs.shape[0]
    d_ff = wg.shape[2]
    n_f = d_ff // tf
    f_blk = lambda i, f, nu: jnp.where(i < nu[0], f, n_f - 1)
    grid_spec = pltpu.PrefetchScalarGridSpec(
        num_scalar_prefetch=2,
        grid=(n_rows // tm, n_f),
        in_specs=[
            pl.BlockSpec((tm, D_MODEL // 2), lambda i, f, te, nu: (i, 0)),
            pl.BlockSpec((1, D_MODEL, tf), lambda i, f, te, nu: (te[i], 0, f_blk(i, f, nu))),
            pl.BlockSpec((1, D_MODEL, tf), lambda i, f, te, nu: (te[i], 0, f_blk(i, f, nu))),
            pl.BlockSpec((1, tf, D_MODEL), lambda i, f, te, nu: (te[i], f_blk(i, f, nu), 0)),
        ],
        out_specs=pl.BlockSpec((tm, D_MODEL // 2), lambda i, f, te, nu: (i, 0)),
        scratch_shapes=[pltpu.VMEM((tm, D_MODEL), BF16), pltpu.VMEM((tm, D_MODEL), F32)],
    )
    return pl.pallas_call(
        _expert_kernel,
        grid_spec=grid_spec,
        out_shape=jax.ShapeDtypeStruct((n_rows, D_MODEL // 2), jnp.uint32),
        compiler_params=_cparams("arbitrary", "arbitrary"),
        name="expert_ffn",
    )(tile_expert, n_used, xs, wg, wu, wd)


def _combine_kernel(y0_ref, y1_ref, gt_ref, x_ref, mod_ref, modc_ref, g_ref, b_ref, o_ref, *, tm, row0):
    is_ctx = _row_is_ctx(tm, pl.program_id(1), row0)
    gt = gt_ref[0]
    ffn = gt[:, 0:1] * _unpack_bf16_pairs(y0_ref[0, 0]) + gt[:, 1:2] * _unpack_bf16_pairs(y1_ref[0, 0])
    g2 = _mod_pick(is_ctx, modc_ref, mod_ref, 5)
    o_ref[0] = _layernorm_rows(DEEPNORM_ALPHA * x_ref[0] + (1.0 + g2) * ffn, g_ref[...], b_ref[...])


def _moe_combine(yg, gates, x1, mod, ln_g, ln_b, *, tm, row0):
    b, rows, _ = x1.shape
    row_spec = lambda n: pl.BlockSpec((1, tm, n), lambda i, j: (i, j, 0))
    vec = lambda n: pl.BlockSpec((1, n), lambda i, j: (0, 0))
    return pl.pallas_call(
        functools.partial(_combine_kernel, tm=tm, row0=row0),
        grid=(b, rows // tm),
        in_specs=[
            pl.BlockSpec((1, 1, tm, D_MODEL // 2), lambda i, j: (0, i, j, 0)),
            pl.BlockSpec((1, 1, tm, D_MODEL // 2), lambda i, j: (1, i, j, 0)),
            row_spec(LANES), row_spec(D_MODEL),
            pl.BlockSpec((1, 1, 6 * D_MODEL), lambda i, j: (i, 0, 0)),
            pl.BlockSpec((1, 1, 6 * D_MODEL), lambda i, j: (MOD_ROWS - 1, 0, 0)),
            vec(D_MODEL), vec(D_MODEL),
        ],
        out_specs=row_spec(D_MODEL),
        out_shape=jax.ShapeDtypeStruct((b, rows, D_MODEL), F32),
        compiler_params=_cparams("arbitrary", "arbitrary"),
        name="moe_combine",
    )(yg, yg, gates, x1, mod, mod, ln_g.reshape(1, -1), ln_b.reshape(1, -1))


def _routing_plan(eidx, *, tm):
    t = eidx.shape[0]
    n_tiles = (t * TOP_K) // tm + N_EXPERTS
    flat = eidx.reshape(-1)
    onehot = (flat[:, None] == jnp.arange(N_EXPERTS, dtype=jnp.int32)[None, :]).astype(jnp.int32)
    csum = jnp.cumsum(onehot, axis=0)
    counts = csum[-1]
    rank = jnp.sum((csum - onehot) * onehot, axis=1)
    tiles = (counts + tm - 1) // tm
    tile_end = jnp.cumsum(tiles)
    tile_start = tile_end - tiles
    pos = jnp.sum(onehot * tile_start[None, :], axis=1) * tm + rank
    src = jnp.zeros((n_tiles * tm,), jnp.int32).at[pos].set(jnp.arange(t * TOP_K, dtype=jnp.int32) // TOP_K)
    n_used = tile_end[-1]
    tile_id = jnp.minimum(jnp.arange(n_tiles, dtype=jnp.int32), n_used - 1)
    tile_expert = jnp.sum((tile_id[:, None] >= tile_end[None, :]).astype(jnp.int32), axis=1)
    return pos.astype(jnp.int32), src, tile_expert.astype(jnp.int32), n_used.reshape(1).astype(jnp.int32)


def _moe_ffn(h2, eidx, gates, x1, mod, wg, wu, wd, ln_g, ln_b, *, first_expert, row0, tm_row, tm_expert, tf):
    b, rows, _ = x1.shape
    t = b * rows
    pos, src, tile_expert, n_used = _routing_plan(eidx.reshape(t, LANES)[:, 0:TOP_K], tm=tm_expert)
    xs = _gather_rows(h2.reshape(t, D_MODEL // 2), src)
    y = _expert_ffn(xs, tile_expert + first_expert, n_used, wg, wu, wd, tm=tm_expert, tf=tf)
    yg = _gather_rows(y, pos.reshape(t, TOP_K).T.reshape(-1)).reshape(TOP_K, b, rows, D_MODEL // 2)
    return _moe_combine(yg, gates, x1, mod, ln_g, ln_b, tm=tm_row, row0=row0)


TM_ROW = 768
TM_LAST = 256
TF_DENSE = 1408
TM_EXPERT = 1024
TF_EXPERT = 512
SC_CORES = 2
SC_SUBCORES = 16
GATHER_CHUNK = 64


def kernel(x, c, ctx, c_ctx, w_mod, b_mod, w_in, w_out, na_rpb, conv_w, conv_b, dt_bias, a_log, d_skip,
           ssd_norm_w, ln_g, ln_b, ffn_w_gate, ffn_w_up, ffn_w_down, router_w, moe_w_gate, moe_w_up,
           moe_w_down):
    b, seq, _ = x.shape
    rows_lat = seq // GRID_W
    xa = jnp.concatenate([ctx, x], axis=1)
    cvec = jnp.zeros((MOD_ROWS, D_MODEL), F32).at[0:b].set(c).at[MOD_ROWS - 1].set(c_ctx)
    mod_all = _modulation(cvec, w_mod, b_mod)
    win_base, win_cfg_id, win_cfgs = _attn_window_plan(rows_lat)
    cos_t, sin_t = _rope_tables(seq)
    sel = _head_select_matrix()
    w_in_p = jnp.pad(w_in, ((0, 0), (0, 0), (0, D_IN_PAD - D_IN_PROJ))).astype(BF16)
    w_out16 = w_out.astype(BF16)

    for layer in range(DEPTH):
        last = layer == DEPTH - 1
        is_moe = layer % 2 == 1
        mod = mod_all[layer].reshape(MOD_ROWS, 1, 6 * D_MODEL)
        qkv, z, xbc, dt = _in_proj(xa, mod, w_in_p[layer], tm=TM_ROW)
        a = _attention(qkv, _expand_rel_bias(na_rpb[layer], win_cfgs), jnp.asarray(win_base),
                       jnp.asarray(win_cfg_id), with_ctx_out=not last)
        y = _ssd_scan(xbc, dt, conv_w[layer], conv_b[layer], dt_bias[layer], a_log[layer], d_skip[layer],
                      cos_t, sin_t, sel)
        router_p = None
        if is_moe:
            r_pad = jnp.pad(router_w[layer // 2], ((0, 0), (0, LANES - N_EXPERTS)))
            r_hi = r_pad.astype(BF16)
            r_mid = (r_pad - r_hi.astype(F32)).astype(BF16)
            router_p = jnp.concatenate([r_hi, r_mid, r_hi], axis=0)
        row0, tm_post = (CTX_LEN, TM_LAST) if last else (0, TM_ROW)
        outs = _out_proj(a, y, z, xa, mod, w_out16[layer], ssd_norm_w[layer], ln_g[layer, 0], ln_b[layer, 0],
                         router_p, tm=tm_post, row0=row0)
        if is_moe:
            x1, h2, eidx, gates = outs
            m = layer // 2
            xa = _moe_ffn(h2, eidx, gates, x1, mod, moe_w_gate.reshape(-1, *moe_w_gate.shape[2:]),
                          moe_w_up.reshape(-1, *moe_w_up.shape[2:]), moe_w_down.reshape(-1, *moe_w_down.shape[2:]),
                          ln_g[layer, 1], ln_b[layer, 1], first_expert=m * N_EXPERTS, row0=row0,
                          tm_row=tm_post, tm_expert=TM_EXPERT, tf=TF_EXPERT)
        else:
            assert not last
            x1, h2 = outs
            m = layer // 2
            xa = _dense_ffn(h2, x1, mod, ffn_w_gate[m].astype(BF16), ffn_w_up[m].astype(BF16),
                            ffn_w_down[m].astype(BF16), ln_g[layer, 1], ln_b[layer, 1], tm=TM_ROW, tf=TF_DENSE)
    return xa
```

```python
import functools
import math

import numpy as np
import jax
import jax.numpy as jnp
from jax import lax
from jax.experimental import pallas as pl
from jax.experimental.pallas import tpu as pltpu
from jax.experimental.pallas import tpu_sc as plsc

F32 = jnp.float32
BF16 = jnp.bfloat16
HIGHEST = lax.Precision.HIGHEST

D_MODEL = 1024
DEPTH = 4
GRID_W = 64
CTX_LEN = 256
N_HEADS_NA = 8
HEAD_DIM_NA = 64
D_NA = N_HEADS_NA * HEAD_DIM_NA
WIN_R = 8
WIN_C = 16
N_HEADS_SSD = 8
HEAD_DIM_SSD = 64
D_SSD = N_HEADS_SSD * HEAD_DIM_SSD
N_GROUPS_SSD = 2
D_STATE = 128
D_CONV = 5
CHUNK = 128
ROPE_BASE = 10000.0
GN = N_GROUPS_SSD * D_STATE
CONV_CH = D_SSD + 2 * GN
D_IN_PROJ = 3 * D_NA + D_SSD + CONV_CH + 2 * N_HEADS_SSD
N_EXPERTS = 8
TOP_K = 2
DEEPNORM_ALPHA = (2.0 * DEPTH) ** 0.25
LN_EPS = 1e-5
RMS_EPS = 1e-5

LANES = 128
SUBLANES = 8
VMEM_LIMIT = 56 * 1024 * 1024
DT_PAD = LANES
D_IN_PAD = D_IN_PROJ - 2 * N_HEADS_SSD + DT_PAD
OFF_Z = 3 * D_NA
OFF_XBC = OFF_Z + D_SSD
OFF_DT = OFF_XBC + CONV_CH
MOD_ROWS = 24
HEAD_PAIR = 2 * HEAD_DIM_NA
ROWS_PER_STEP = 2
WIN_ROWS = WIN_R + ROWS_PER_STEP - 1


def _cparams(*sem):
    return pltpu.CompilerParams(dimension_semantics=sem, vmem_limit_bytes=VMEM_LIMIT)


def _silu(v):
    return v * jax.nn.sigmoid(v)


def _layernorm_rows(v, g, b):
    mu = jnp.mean(v, axis=-1, keepdims=True)
    d = v - mu
    var = jnp.mean(d * d, axis=-1, keepdims=True)
    return d * lax.rsqrt(var + LN_EPS) * g + b


def _pack_bf16_pairs(v):
    n = v.shape[1] // 2
    lo = pltpu.bitcast(v[:, 0:n].astype(BF16).astype(F32), jnp.uint32)
    hi = pltpu.bitcast(v[:, n:2 * n].astype(BF16).astype(F32), jnp.uint32)
    return hi | (lo >> 16)


def _unpack_bf16_pairs(w):
    lo = pltpu.bitcast(w << 16, F32)
    hi = pltpu.bitcast(w & jnp.uint32(0xFFFF0000), F32)
    return jnp.concatenate([lo, hi], axis=1)


def _row_is_ctx(tm, j, row0=0):
    row = lax.broadcasted_iota(jnp.int32, (tm, 1), 0) + j * tm + row0
    return row < CTX_LEN


def _mod_pick(is_ctx, modc_ref, mod_ref, seg):
    lo, hi = seg * D_MODEL, (seg + 1) * D_MODEL
    return jnp.where(is_ctx, modc_ref[0, :, lo:hi], mod_ref[0, :, lo:hi])


def _mod_kernel(c_ref, w_ref, b_ref, o_ref):
    cs = _silu(c_ref[...])
    o_ref[0] = jnp.dot(cs, w_ref[0], preferred_element_type=F32, precision=HIGHEST) + b_ref[0]


def _modulation(cvec, w_mod, b_mod):
    tn = D_MODEL
    return pl.pallas_call(
        _mod_kernel,
        grid=(DEPTH, 6 * D_MODEL // tn),
        in_specs=[
            pl.BlockSpec((MOD_ROWS, D_MODEL), lambda l, n: (0, 0)),
            pl.BlockSpec((1, D_MODEL, tn), lambda l, n: (l, 0, n)),
            pl.BlockSpec((1, 1, tn), lambda l, n: (l, 0, n)),
        ],
        out_specs=pl.BlockSpec((1, MOD_ROWS, tn), lambda l, n: (l, 0, n)),
        out_shape=jax.ShapeDtypeStruct((DEPTH, MOD_ROWS, 6 * D_MODEL), F32),
        compiler_params=_cparams("arbitrary", "arbitrary"),
        name="modulation",
    )(cvec, w_mod, b_mod.reshape(DEPTH, 1, 6 * D_MODEL))


def _in_proj_kernel(x_ref, mod_ref, modc_ref, w_ref, qkv_ref, z_ref, xbc_ref, dt_ref, *, tm):
    is_ctx = _row_is_ctx(tm, pl.program_id(1))
    sh = _mod_pick(is_ctx, modc_ref, mod_ref, 0)
    sc = _mod_pick(is_ctx, modc_ref, mod_ref, 1)
    h = (x_ref[0] * (1.0 + sc) + sh).astype(BF16)
    qkv_ref[0] = jnp.dot(h, w_ref[:, 0:OFF_Z], preferred_element_type=F32).astype(BF16)
    z_ref[0] = jnp.dot(h, w_ref[:, OFF_Z:OFF_XBC], preferred_element_type=F32)
    xbc_ref[0] = jnp.dot(h, w_ref[:, OFF_XBC:OFF_DT], preferred_element_type=F32)
    dt_ref[0] = jnp.dot(h, w_ref[:, OFF_DT:D_IN_PAD], preferred_element_type=F32)


def _in_proj(xa, mod, w_in_p, *, tm):
    b, rows, _ = xa.shape
    row_spec = lambda n: pl.BlockSpec((1, tm, n), lambda i, j: (i, j, 0))
    return pl.pallas_call(
        functools.partial(_in_proj_kernel, tm=tm),
        grid=(b, rows // tm),
        in_specs=[
            row_spec(D_MODEL),
            pl.BlockSpec((1, 1, 6 * D_MODEL), lambda i, j: (i, 0, 0)),
            pl.BlockSpec((1, 1, 6 * D_MODEL), lambda i, j: (MOD_ROWS - 1, 0, 0)),
            pl.BlockSpec((D_MODEL, D_IN_PAD), lambda i, j: (0, 0)),
        ],
        out_specs=[row_spec(OFF_Z), row_spec(D_SSD), row_spec(CONV_CH), row_spec(DT_PAD)],
        out_shape=[
            jax.ShapeDtypeStruct((b, rows, OFF_Z), BF16),
            jax.ShapeDtypeStruct((b, rows, D_SSD), F32),
            jax.ShapeDtypeStruct((b, rows, CONV_CH), F32),
            jax.ShapeDtypeStruct((b, rows, DT_PAD), F32),
        ],
        compiler_params=_cparams("arbitrary", "arbitrary"),
        name="in_proj",
    )(xa, mod, mod, w_in_p)


def _attn_head_pair(q_pair, keys_vals, biases):
    m_rows = q_pair.shape[0]
    lane = lax.broadcasted_iota(jnp.int32, q_pair.shape, 1)
    zero = jnp.zeros_like(q_pair)
    q_stack = jnp.concatenate([jnp.where(lane < HEAD_DIM_NA, q_pair, zero),
                               jnp.where(lane >= HEAD_DIM_NA, q_pair, zero)], axis=0)
    scores = []
    for (k, _), bias in zip(keys_vals, biases):
        s = lax.dot_general(q_stack, k, (((1,), (1,)), ((), ())), preferred_element_type=F32)
        scores.append(s if bias is None else s + bias)
    m = scores[0].max(axis=-1, keepdims=True)
    for s in scores[1:]:
        m = jnp.maximum(m, s.max(axis=-1, keepdims=True))
    denom = jnp.zeros_like(m)
    acc = jnp.zeros((2 * m_rows, HEAD_PAIR), F32)
    for s, (_, v) in zip(scores, keys_vals):
        p = jnp.exp(s - m)
        denom = denom + p.sum(axis=-1, keepdims=True)
        acc = acc + jnp.dot(p.astype(BF16), v, preferred_element_type=F32)
    out = acc / denom
    return jnp.where(lane < HEAD_DIM_NA, out[0:m_rows], out[m_rows:2 * m_rows])


def _attn_kernel(base_ref, cfg_ref, qkv_ref, bias_ref, o_ref, *, rows_lat, with_ctx_out):
    scale = HEAD_DIM_NA ** -0.5
    k_off, v_off = D_NA, 2 * D_NA
    n_pairs = N_HEADS_NA // 2
    q_rows = ROWS_PER_STEP * GRID_W
    win = WIN_ROWS * GRID_W

    def lat_rows(j, carry):
        q_base = pl.multiple_of(CTX_LEN + j * q_rows, q_rows)
        k_base = pl.multiple_of(CTX_LEN + base_ref[j] * GRID_W, GRID_W)
        cfg = cfg_ref[j]
        for p in range(n_pairs):
            lo = p * HEAD_PAIR
            q = qkv_ref[0, pl.ds(q_base, q_rows), lo:lo + HEAD_PAIR] * scale
            k_loc = qkv_ref[0, pl.ds(k_base, win), k_off + lo:k_off + lo + HEAD_PAIR]
            v_loc = qkv_ref[0, pl.ds(k_base, win), v_off + lo:v_off + lo + HEAD_PAIR]
            k_ctx = qkv_ref[0, 0:CTX_LEN, k_off + lo:k_off + lo + HEAD_PAIR]
            v_ctx = qkv_ref[0, 0:CTX_LEN, v_off + lo:v_off + lo + HEAD_PAIR]
            o = _attn_head_pair(q, [(k_loc, v_loc), (k_ctx, v_ctx)], [bias_ref[cfg, p], None])
            o_ref[0, pl.ds(q_base, q_rows), lo:lo + HEAD_PAIR] = o.astype(o_ref.dtype)
        return carry

    lax.fori_loop(0, rows_lat // ROWS_PER_STEP, lat_rows, 0)

    if with_ctx_out:
        for p in range(n_pairs):
            lo = p * HEAD_PAIR
            q = qkv_ref[0, 0:CTX_LEN, lo:lo + HEAD_PAIR] * scale
            k_ctx = qkv_ref[0, 0:CTX_LEN, k_off + lo:k_off + lo + HEAD_PAIR]
            v_ctx = qkv_ref[0, 0:CTX_LEN, v_off + lo:v_off + lo + HEAD_PAIR]
            o = _attn_head_pair(q, [(k_ctx, v_ctx)], [None])
            o_ref[0, 0:CTX_LEN, lo:lo + HEAD_PAIR] = o.astype(o_ref.dtype)
    else:
        o_ref[0, 0:CTX_LEN, :] = jnp.zeros((CTX_LEN, D_NA), o_ref.dtype)


def _attention(qkv, bias, win_base, cfg, *, with_ctx_out):
    b, rows, _ = qkv.shape
    rows_lat = (rows - CTX_LEN) // GRID_W
    grid_spec = pltpu.PrefetchScalarGridSpec(
        num_scalar_prefetch=2,
        grid=(b,),
        in_specs=[
            pl.BlockSpec((1, rows, OFF_Z), lambda i, wb, cf: (i, 0, 0)),
            pl.BlockSpec(bias.shape, lambda i, wb, cf: (0, 0, 0, 0)),
        ],
        out_specs=pl.BlockSpec((1, rows, D_NA), lambda i, wb, cf: (i, 0, 0)),
    )
    return pl.pallas_call(
        functools.partial(_attn_kernel, rows_lat=rows_lat, with_ctx_out=with_ctx_out),
        grid_spec=grid_spec,
        out_shape=jax.ShapeDtypeStruct((b, rows, D_NA), BF16),
        compiler_params=_cparams("arbitrary"),
        name="attention",
    )(win_base, cfg, qkv, bias)


def _attn_window_plan(rows_lat):
    kr = min(WIN_R, rows_lat)
    assert rows_lat % ROWS_PER_STEP == 0 and rows_lat >= WIN_ROWS and kr == WIN_R
    bases, cfg_ids, configs = [], [], []
    for i in range(0, rows_lat, ROWS_PER_STEP):
        r0 = [int(np.clip(i + s - kr // 2, 0, rows_lat - kr)) for s in range(ROWS_PER_STEP)]
        base = min(r0[0], rows_lat - WIN_ROWS)
        assert all(0 <= r - base and r - base + kr <= WIN_ROWS for r in r0)
        c = tuple((i + s - r0[s], r0[s] - base) for s in range(ROWS_PER_STEP))
        if c not in configs:
            configs.append(c)
        bases.append(base)
        cfg_ids.append(configs.index(c))
    return np.asarray(bases, np.int32), np.asarray(cfg_ids, np.int32), configs


def _expand_rel_bias(rpb, configs):
    cols = np.arange(GRID_W)
    col_start = np.clip(cols - WIN_C // 2, 0, GRID_W - WIN_C)
    kc = cols[None, :]
    in_win = (kc >= col_start[:, None]) & (kc < col_start[:, None] + WIN_C)
    col_idx = np.clip(kc - cols[:, None] + WIN_C - 1, 0, 2 * WIN_C - 2)
    d = np.asarray([[c[s][0] for s in range(ROWS_PER_STEP)] for c in configs])[:, :, None]
    o = np.asarray([[c[s][1] for s in range(ROWS_PER_STEP)] for c in configs])[:, :, None]
    r = np.arange(WIN_ROWS)[None, None, :] - o
    row_ok = (r >= 0) & (r < WIN_R)
    row_idx = np.clip(r - d + WIN_R - 1, 0, 2 * WIN_R - 2)
    t = rpb[:, row_idx]
    t = t[..., col_idx]
    ok = row_ok[None, :, :, :, None, None] & in_win[None, None, None, None]
    t = jnp.where(jnp.asarray(ok), t.astype(F32), -jnp.inf)
    t = t.transpose(1, 0, 2, 4, 3, 5)
    return t.reshape(len(configs), N_HEADS_NA // 2, 2 * ROWS_PER_STEP * GRID_W, WIN_ROWS * GRID_W)


def _split3(v):
    hi = v.astype(BF16)
    rest = v - hi.astype(F32)
    mid = rest.astype(BF16)
    lo = (rest - mid.astype(F32)).astype(BF16)
    return hi, mid, lo


def _ssd_kernel(xbc_ref, dt_ref, convw_ref, convb_ref, dtb_ref, alog_ref, dskip_ref, cos_ref, sin_ref,
                sel_ref, y_ref, htf_ref, htb_ref, stf_ref, c_ref, ef_ref, ext_ref, *, n_chunks):
    n_ctx_chunks = CTX_LEN // CHUNK
    hp = D_SSD // N_GROUPS_SSD
    htf_ref[...] = jnp.zeros_like(htf_ref)
    htb_ref[...] = jnp.zeros_like(htb_ref)

    li = lax.broadcasted_iota(jnp.int32, (CHUNK, CHUNK), 0)
    si = lax.broadcasted_iota(jnp.int32, (CHUNK, CHUNK), 1)
    lower = li >= si
    upper = li <= si
    tri = jnp.concatenate([lower, upper], axis=0).astype(F32).astype(BF16)
    lane_c = lax.broadcasted_iota(jnp.int32, (CHUNK, LANES), 1)
    rope_lo = (lane_c % (D_STATE // 2)) < (D_STATE // 4)
    pair_lane = lax.broadcasted_iota(jnp.int32, (CHUNK, HEAD_PAIR), 1)

    def rope(w, cos, sin):
        partner = jnp.where(rope_lo, pltpu.roll(w, LANES - D_STATE // 4, 1), pltpu.roll(w, D_STATE // 4, 1))
        return w * cos + partner * sin

    def local_pass(k, carry):
        c = jnp.where(k < n_ctx_chunks, n_ctx_chunks - 1 - k, n_chunks + n_ctx_chunks - 1 - k)
        base = pl.multiple_of(c * CHUNK, CHUNK)
        is_start = (c == 0) | (c == n_ctx_chunks)
        is_end = (c == n_ctx_chunks - 1) | (c == n_chunks - 1)
        prev_base = pl.multiple_of(jnp.maximum(base - SUBLANES, 0), SUBLANES)
        next_base = pl.multiple_of(jnp.minimum(base + CHUNK, (n_chunks - 1) * CHUNK + CHUNK - SUBLANES), SUBLANES)
        ext_ref[0:SUBLANES, :] = jnp.where(is_start, 0.0, xbc_ref[0, pl.ds(prev_base, SUBLANES), :])
        ext_ref[SUBLANES:SUBLANES + CHUNK, :] = xbc_ref[0, pl.ds(base, CHUNK), :]
        ext_ref[SUBLANES + CHUNK:2 * SUBLANES + CHUNK, :] = jnp.where(
            is_end, 0.0, xbc_ref[0, pl.ds(next_base, SUBLANES), :])
        ext = ext_ref[...]
        conv = convb_ref[...]
        for t in range(D_CONV):
            shift = (D_CONV // 2 - t) % ext.shape[0]
            tap = ext if shift == 0 else pltpu.roll(ext, shift, 0)
            conv = conv + tap[SUBLANES:SUBLANES + CHUNK] * convw_ref[t:t + 1, :]
        u = _silu(conv)
        xs = u[:, 0:D_SSD]
        cos = cos_ref[pl.ds(base, CHUNK), :]
        sin = sin_ref[pl.ds(base, CHUNK), :]
        bm = [rope(u[:, D_SSD + g * D_STATE:D_SSD + (g + 1) * D_STATE], cos, sin) for g in range(N_GROUPS_SSD)]
        cm = [rope(u[:, D_SSD + GN + g * D_STATE:D_SSD + GN + (g + 1) * D_STATE], cos, sin)
              for g in range(N_GROUPS_SSD)]

        dt = jax.nn.softplus(dt_ref[0, pl.ds(base, CHUNK), :] + dtb_ref[...])
        a = dt * (-jnp.exp(alog_ref[...]))
        sums = jnp.dot(tri, jnp.concatenate(_split3(a), axis=1), preferred_element_type=F32)
        sums = sums[:, 0:LANES] + sums[:, LANES:2 * LANES] + sums[:, 2 * LANES:3 * LANES]
        acs = jnp.where(lane_c < N_HEADS_SSD, sums[0:CHUNK], sums[CHUNK:2 * CHUNK])
        acs_t = acs.T
        dt_t = dt.T
        wide = jnp.dot(jnp.concatenate(_split3(jnp.concatenate([acs, dt], axis=0)), axis=1), sel_ref[...],
                       preferred_element_type=F32)
        acs_x, dt_x = wide[0:CHUNK], wide[CHUNK:2 * CHUNK]
        arg_f, arg_b = acs_x[:, 0:D_SSD], acs_x[:, D_SSD:2 * D_SSD]
        e_f = jnp.exp(arg_f)
        e_b = jnp.exp(arg_b)
        x_f = (xs * dt_x[:, 0:D_SSD] * jnp.exp(arg_f[CHUNK - 1:CHUNK, :] - arg_f)).astype(BF16)
        x_b = (xs * dt_x[:, D_SSD:2 * D_SSD] * jnp.exp(arg_b[0:1, :] - arg_b)).astype(BF16)

        xs16 = xs.astype(BF16)
        y_parts = []
        for g in range(N_GROUPS_SSD):
            bt = bm[g].T.astype(BF16)
            c16 = cm[g].astype(BF16)
            gmat = jnp.dot(c16, bt, preferred_element_type=F32)
            gs = slice(g * hp, (g + 1) * hp)
            heads = range(g * (N_HEADS_SSD // N_GROUPS_SSD), (g + 1) * (N_HEADS_SSD // N_GROUPS_SSD))
            ms = []
            for h in heads:
                jf, jb = h, N_HEADS_SSD + h
                wf = jnp.where(lower, jnp.exp(acs[:, jf:jf + 1] - acs_t[jf:jf + 1, :]), 0.0) * dt_t[jf:jf + 1, :]
                wb = jnp.where(upper, jnp.exp(acs[:, jb:jb + 1] - acs_t[jb:jb + 1, :]), 0.0) * dt_t[jb:jb + 1, :]
                ms.append((gmat * (wf + wb)).astype(BF16))
            y_g = []
            for q in range(len(ms) // 2):
                lo = g * hp + q * HEAD_PAIR
                xp = xs16[:, lo:lo + HEAD_PAIR]
                rhs = jnp.concatenate([jnp.where(pair_lane < HEAD_DIM_SSD, xp, jnp.zeros_like(xp)),
                                       jnp.where(pair_lane >= HEAD_DIM_SSD, xp, jnp.zeros_like(xp))], axis=0)
                lhs = jnp.concatenate([ms[2 * q], ms[2 * q + 1]], axis=1)
                y_g.append(jnp.dot(lhs, rhs, preferred_element_type=F32))
            y_diag = jnp.concatenate(y_g, axis=1)
            y_off = jnp.dot(c16, htb_ref[g].astype(BF16), preferred_element_type=F32) * e_b[:, gs]
            y_parts.append(y_diag + y_off)
            htb_ref[g] = htb_ref[g] * e_b[0:1, gs] + jnp.dot(bt, x_b[:, gs], preferred_element_type=F32)
            stf_ref[c, g] = jnp.dot(bt, x_f[:, gs], preferred_element_type=F32)
            c_ref[c, g] = c16
        ef_ref[c] = e_f
        y_ref[0, pl.ds(base, CHUNK), :] = jnp.concatenate(y_parts, axis=1) + dskip_ref[...] * xs
        return carry

    lax.fori_loop(0, n_chunks, local_pass, 0)

    def forward_pass(c, carry):
        base = pl.multiple_of(c * CHUNK, CHUNK)
        e_f = ef_ref[c]
        y_parts = []
        for g in range(N_GROUPS_SSD):
            gs = slice(g * hp, (g + 1) * hp)
            y_parts.append(jnp.dot(c_ref[c, g], htf_ref[g].astype(BF16), preferred_element_type=F32) * e_f[:, gs])
            htf_ref[g] = htf_ref[g] * e_f[CHUNK - 1:CHUNK, gs] + stf_ref[c, g]
        y_ref[0, pl.ds(base, CHUNK), :] = y_ref[0, pl.ds(base, CHUNK), :] + jnp.concatenate(y_parts, axis=1)
        return carry

    lax.fori_loop(0, n_chunks, forward_pass, 0)


def _ssd_scan(xbc, dt, conv_w, conv_b, dt_bias, a_log, d_skip, cos_t, sin_t, sel):
    b, rows, _ = xbc.shape
    n_chunks = rows // CHUNK
    hp = D_SSD // N_GROUPS_SSD
    full = lambda a: pl.BlockSpec(a.shape, lambda i: (0,) * a.ndim)
    pad16 = lambda v: jnp.pad(v.reshape(1, -1), ((0, 0), (0, DT_PAD - v.size)))
    conv_w8 = jnp.pad(conv_w, ((0, SUBLANES - D_CONV), (0, 0)))
    params = [conv_w8, conv_b.reshape(1, -1), pad16(dt_bias), pad16(a_log),
              jnp.repeat(d_skip, HEAD_DIM_SSD).reshape(1, -1), cos_t, sin_t, sel]
    return pl.pallas_call(
        functools.partial(_ssd_kernel, n_chunks=n_chunks),
        grid=(b,),
        in_specs=[pl.BlockSpec((1, rows, CONV_CH), lambda i: (i, 0, 0)),
                  pl.BlockSpec((1, rows, DT_PAD), lambda i: (i, 0, 0))] + [full(p) for p in params],
        out_specs=pl.BlockSpec((1, rows, D_SSD), lambda i: (i, 0, 0)),
        out_shape=jax.ShapeDtypeStruct((b, rows, D_SSD), F32),
        scratch_shapes=[
            pltpu.VMEM((N_GROUPS_SSD, D_STATE, hp), F32),
            pltpu.VMEM((N_GROUPS_SSD, D_STATE, hp), F32),
            pltpu.VMEM((n_chunks, N_GROUPS_SSD, D_STATE, hp), F32),
            pltpu.VMEM((n_chunks, N_GROUPS_SSD, CHUNK, D_STATE), BF16),
            pltpu.VMEM((n_chunks, CHUNK, D_SSD), F32),
            pltpu.VMEM((CHUNK + 2 * SUBLANES, CONV_CH), F32),
        ],
        compiler_params=_cparams("arbitrary"),
        name="ssd_scan",
    )(xbc, dt, *params)


def _rope_tables(seq):
    half, quarter = D_STATE // 2, D_STATE // 4
    t = jnp.arange(seq)
    inv_freq = ROPE_BASE ** (-jnp.arange(quarter, dtype=F32) / quarter)
    ang_r = (t // GRID_W).astype(F32)[:, None] * inv_freq
    ang_c = (t % GRID_W).astype(F32)[:, None] * inv_freq
    cos = jnp.concatenate([jnp.cos(ang_r)] * 2 + [jnp.cos(ang_c)] * 2, axis=-1)
    sin = jnp.concatenate([-jnp.sin(ang_r), jnp.sin(ang_r), -jnp.sin(ang_c), jnp.sin(ang_c)], axis=-1)
    cos = jnp.concatenate([jnp.ones((CTX_LEN, 2 * half), F32), cos], axis=0)
    sin = jnp.concatenate([jnp.zeros((CTX_LEN, 2 * half), F32), sin], axis=0)
    return cos, sin


def _head_select_matrix():
    sel = np.zeros((DT_PAD, 2 * D_SSD), np.float32)
    for j in range(2 * N_HEADS_SSD):
        sel[j, j * HEAD_DIM_SSD:(j + 1) * HEAD_DIM_SSD] = 1.0
    return jnp.asarray(np.concatenate([sel] * 3, axis=0), dtype=BF16)


def _out_proj_kernel(*refs, tm, with_router, row0):
    if with_router:
        (a_ref, y_ref, z_ref, x_ref, mod_ref, modc_ref, w_ref, nw_ref, g_ref, b_ref, r_ref,
         x1_ref, h2_ref, ei_ref, gt_ref) = refs
    else:
        a_ref, y_ref, z_ref, x_ref, mod_ref, modc_ref, w_ref, nw_ref, g_ref, b_ref, x1_ref, h2_ref = refs
    is_ctx = _row_is_ctx(tm, pl.program_id(1), row0)
    yg = y_ref[0] * _silu(z_ref[0])
    gw = D_SSD // N_GROUPS_SSD
    parts = []
    for g in range(N_GROUPS_SSD):
        part = yg[:, g * gw:(g + 1) * gw]
        parts.append(part * lax.rsqrt(jnp.mean(part * part, axis=-1, keepdims=True) + RMS_EPS))
    s = (jnp.concatenate(parts, axis=1) * nw_ref[...]).astype(BF16)
    mix = (jnp.dot(a_ref[0], w_ref[0:D_NA, :], preferred_element_type=F32)
           + jnp.dot(s, w_ref[D_NA:D_NA + D_SSD, :], preferred_element_type=F32))
    g1 = _mod_pick(is_ctx, modc_ref, mod_ref, 2)
    x1 = _layernorm_rows(DEEPNORM_ALPHA * x_ref[0] + (1.0 + g1) * mix, g_ref[...], b_ref[...])
    x1_ref[0] = x1
    sh2 = _mod_pick(is_ctx, modc_ref, mod_ref, 3)
    sc2 = _mod_pick(is_ctx, modc_ref, mod_ref, 4)
    h2 = x1 * (1.0 + sc2) + sh2
    if with_router:
        h2_ref[0] = _pack_bf16_pairs(h2)
    else:
        h2_ref[0] = h2.astype(h2_ref.dtype)
    if with_router:
        h_hi = h2.astype(BF16)
        h_mid = (h2 - h_hi.astype(F32)).astype(BF16)
        logits = jnp.dot(jnp.concatenate([h_hi, h_hi, h_mid], axis=1), r_ref[...],
                         preferred_element_type=F32)
        lane = lax.broadcasted_iota(jnp.int32, logits.shape, 1)
        logits = jnp.where(lane < N_EXPERTS, logits, -jnp.inf)
        m1 = logits.max(axis=-1, keepdims=True)
        i1 = jnp.where(logits == m1, lane, LANES).min(axis=-1, keepdims=True)
        rest = jnp.where(lane == i1, -jnp.inf, logits)
        m2 = rest.max(axis=-1, keepdims=True)
        i2 = jnp.where(rest == m2, lane, LANES).min(axis=-1, keepdims=True)
        e2 = jnp.exp(m2 - m1)
        gate1 = 1.0 / (1.0 + e2)
        gate2 = e2 / (1.0 + e2)
        ei_ref[0] = jnp.where(lane == 0, i1, jnp.where(lane == 1, i2, 0))
        gt_ref[0] = jnp.where(lane == 0, gate1, jnp.where(lane == 1, gate2, 0.0))


def _out_proj(a, y, z, xa, mod, w_out, norm_w, ln_g, ln_b, router_p, *, tm, row0=0):
    b, rows_in, _ = xa.shape
    rows = rows_in - row0
    assert row0 % tm == 0 and rows % tm == 0
    with_router = router_p is not None
    row_spec = lambda n: pl.BlockSpec((1, tm, n), lambda i, j: (i, j, 0))
    in_row_spec = lambda n: pl.BlockSpec((1, tm, n), lambda i, j: (i, j + row0 // tm, 0))
    vec = lambda n: pl.BlockSpec((1, n), lambda i, j: (0, 0))
    in_specs = [
        in_row_spec(D_NA), in_row_spec(D_SSD), in_row_spec(D_SSD), in_row_spec(D_MODEL),
        pl.BlockSpec((1, 1, 6 * D_MODEL), lambda i, j: (i, 0, 0)),
        pl.BlockSpec((1, 1, 6 * D_MODEL), lambda i, j: (MOD_ROWS - 1, 0, 0)),
        pl.BlockSpec((D_NA + D_SSD, D_MODEL), lambda i, j: (0, 0)),
        vec(D_SSD), vec(D_MODEL), vec(D_MODEL),
    ]
    args = [a, y, z, xa, mod, mod, w_out, norm_w.reshape(1, -1), ln_g.reshape(1, -1), ln_b.reshape(1, -1)]
    h2_cols, h2_dtype = (D_MODEL // 2, jnp.uint32) if with_router else (D_MODEL, BF16)
    out_specs = [row_spec(D_MODEL), row_spec(h2_cols)]
    out_shape = [jax.ShapeDtypeStruct((b, rows, D_MODEL), F32), jax.ShapeDtypeStruct((b, rows, h2_cols), h2_dtype)]
    if with_router:
        in_specs.append(pl.BlockSpec((3 * D_MODEL, LANES), lambda i, j: (0, 0)))
        args.append(router_p)
        out_specs += [row_spec(LANES), row_spec(LANES)]
        out_shape += [jax.ShapeDtypeStruct((b, rows, LANES), jnp.int32), jax.ShapeDtypeStruct((b, rows, LANES), F32)]
    return pl.pallas_call(
        functools.partial(_out_proj_kernel, tm=tm, with_router=with_router, row0=row0),
        grid=(b, rows // tm),
        in_specs=in_specs,
        out_specs=out_specs,
        out_shape=out_shape,
        compiler_params=_cparams("arbitrary", "arbitrary"),
        name="out_proj",
    )(*args)


def _ffn_kernel(h_ref, x_ref, mod_ref, modc_ref, wg_ref, wu_ref, wd_ref, g_ref, b_ref, o_ref, *, tm, tf):
    is_ctx = _row_is_ctx(tm, pl.program_id(1))
    h = h_ref[0]
    d_ff = wg_ref.shape[1]
    acc = jnp.zeros((tm, D_MODEL), F32)
    for f in range(d_ff // tf):
        fs = slice(f * tf, (f + 1) * tf)
        gate = jnp.dot(h, wg_ref[:, fs], preferred_element_type=F32)
        up = jnp.dot(h, wu_ref[:, fs], preferred_element_type=F32)
        acc = acc + jnp.dot((_silu(gate) * up).astype(BF16), wd_ref[fs, :], preferred_element_type=F32)
    g2 = _mod_pick(is_ctx, modc_ref, mod_ref, 5)
    o_ref[0] = _layernorm_rows(DEEPNORM_ALPHA * x_ref[0] + (1.0 + g2) * acc, g_ref[...], b_ref[...])


def _dense_ffn(h2, x1, mod, wg, wu, wd, ln_g, ln_b, *, tm, tf):
    b, rows, _ = x1.shape
    d_ff = wg.shape[1]
    row_spec = lambda n: pl.BlockSpec((1, tm, n), lambda i, j: (i, j, 0))
    vec = lambda n: pl.BlockSpec((1, n), lambda i, j: (0, 0))
    const = lambda s: pl.BlockSpec(s, lambda i, j: (0, 0), pipeline_mode=pl.Buffered(1))
    return pl.pallas_call(
        functools.partial(_ffn_kernel, tm=tm, tf=tf),
        grid=(b, rows // tm),
        in_specs=[
            row_spec(D_MODEL), row_spec(D_MODEL),
            pl.BlockSpec((1, 1, 6 * D_MODEL), lambda i, j: (i, 0, 0)),
            pl.BlockSpec((1, 1, 6 * D_MODEL), lambda i, j: (MOD_ROWS - 1, 0, 0)),
            const((D_MODEL, d_ff)), const((D_MODEL, d_ff)), const((d_ff, D_MODEL)),
            vec(D_MODEL), vec(D_MODEL),
        ],
        out_specs=row_spec(D_MODEL),
        out_shape=jax.ShapeDtypeStruct((b, rows, D_MODEL), F32),
        compiler_params=_cparams("arbitrary", "arbitrary"),
        name="dense_ffn",
    )(h2, x1, mod, mod, wg, wu, wd, ln_g.reshape(1, -1), ln_b.reshape(1, -1))


def _gather_rows(table, idx):
    n, d = idx.shape[0], table.shape[1]
    workers = SC_CORES * SC_SUBCORES
    n_chunks = n // (workers * GATHER_CHUNK)
    assert n == workers * n_chunks * GATHER_CHUNK
    per_worker = n_chunks * GATHER_CHUNK
    mesh = plsc.VectorSubcoreMesh(core_axis_name="c", subcore_axis_name="s")

    assert n_chunks % 2 == 0

    def body(table_hbm, idx_hbm, out_hbm, idx_v, rows_v, sems):
        wid = lax.axis_index("s") * SC_CORES + lax.axis_index("c")
        pltpu.sync_copy(idx_hbm.at[wid], idx_v)
        base = wid * per_worker

        def gather(j, slot):
            return pltpu.make_async_copy(table_hbm.at[idx_v.at[j]], rows_v.at[slot], sems.at[slot])

        gather(0, 0).start()

        @pl.loop(0, n_chunks, step=2)
        def _(j0):
            for slot in range(2):
                j = j0 + slot
                gather(j, slot).wait()

                @pl.when(j + 1 < n_chunks)
                def _():
                    gather(j + 1, 1 - slot).start()

                pltpu.sync_copy(rows_v.at[slot], out_hbm.at[pl.ds(base + j * GATHER_CHUNK, GATHER_CHUNK)])

    return pl.kernel(
        body,
        out_type=jax.ShapeDtypeStruct((n, d), table.dtype),
        mesh=mesh,
        scratch_types=[
            pltpu.VMEM((n_chunks, GATHER_CHUNK), jnp.int32),
            pltpu.VMEM((2, GATHER_CHUNK, d), table.dtype),
            pltpu.SemaphoreType.DMA((2,)),
        ],
        name="gather_rows",
    )(table, idx.reshape(workers, n_chunks, GATHER_CHUNK))


def _scatter_rows(table, pos_slots, n_out):
    t, d = table.shape
    n_slots = pos_slots.shape[0]
    workers = SC_CORES * SC_SUBCORES
    n_chunks = t // (workers * GATHER_CHUNK)
    assert t == workers * n_chunks * GATHER_CHUNK and n_chunks % 2 == 0
    per_worker = n_chunks * GATHER_CHUNK
    mesh = plsc.VectorSubcoreMesh(core_axis_name="c", subcore_axis_name="s")

    def body(table_hbm, pos_hbm, out_hbm, pos_v, rows_v, sems):
        wid = lax.axis_index("s") * SC_CORES + lax.axis_index("c")
        for k in range(n_slots):
            pltpu.sync_copy(pos_hbm.at[k, wid], pos_v.at[k])
        base = wid * per_worker

        def read(j, slot):
            return pltpu.make_async_copy(table_hbm.at[pl.ds(base + j * GATHER_CHUNK, GATHER_CHUNK)],
                                         rows_v.at[slot], sems.at[slot])

        read(0, 0).start()

        @pl.loop(0, n_chunks, step=2)
        def _(j0):
            for slot in range(2):
                j = j0 + slot
                read(j, slot).wait()

                @pl.when(j + 1 < n_chunks)
                def _():
                    read(j + 1, 1 - slot).start()

                for k in range(n_slots):
                    pltpu.sync_copy(rows_v.at[slot], out_hbm.at[pos_v.at[k, j]])

    return pl.kernel(
        body,
        out_type=jax.ShapeDtypeStruct((n_out, d), table.dtype),
        mesh=mesh,
        scratch_types=[
            pltpu.VMEM((n_slots, n_chunks, GATHER_CHUNK), jnp.int32),
            pltpu.VMEM((2, GATHER_CHUNK, d), table.dtype),
            pltpu.SemaphoreType.DMA((2,)),
        ],
        name="scatter_rows",
    )(table, pos_slots.reshape(n_slots, workers, n_chunks, GATHER_CHUNK))


def _expert_kernel(te_ref, tv_ref, nu_ref, x_ref, wg_ref, wu_ref, wd_ref, y_ref, x16_ref, acc_ref):
    i, f = pl.program_id(0), pl.program_id(1)
    live = i < nu_ref[0]

    @pl.when(f == 0)
    def _():
        acc_ref[...] = jnp.zeros_like(acc_ref)
        row = lax.broadcasted_iota(jnp.int32, (x_ref.shape[0], 1), 0)
        x = jnp.where(row < tv_ref[i], _unpack_bf16_pairs(x_ref[...]), 0.0)
        x16_ref[...] = x.astype(BF16)

    @pl.when(live)
    def _():
        h = x16_ref[...]
        gate = jnp.dot(h, wg_ref[0].astype(BF16), preferred_element_type=F32)
        up = jnp.dot(h, wu_ref[0].astype(BF16), preferred_element_type=F32)
        acc_ref[...] += jnp.dot((_silu(gate) * up).astype(BF16), wd_ref[0].astype(BF16),
                                preferred_element_type=F32)

    @pl.when(f == pl.num_programs(1) - 1)
    def _():
        y_ref[...] = _pack_bf16_pairs(acc_ref[...])


def _expert_ffn(xs, tile_expert, tile_valid, n_used, wg, wu, wd, *, tm, tf):
    n_rows = xs.shape[0]
    d_ff = wg.shape[2]
    n_f = d_ff // tf
    f_blk = lambda i, f, nu: jnp.where(i < nu[0], f, n_f - 1)
    grid_spec = pltpu.PrefetchScalarGridSpec(
        num_scalar_prefetch=3,
        grid=(n_rows // tm, n_f),
        in_specs=[
            pl.BlockSpec((tm, D_MODEL // 2), lambda i, f, te, tv, nu: (i, 0)),
            pl.BlockSpec((1, D_MODEL, tf), lambda i, f, te, tv, nu: (te[i], 0, f_blk(i, f, nu))),
            pl.BlockSpec((1, D_MODEL, tf), lambda i, f, te, tv, nu: (te[i], 0, f_blk(i, f, nu))),
            pl.BlockSpec((1, tf, D_MODEL), lambda i, f, te, tv, nu: (te[i], f_blk(i, f, nu), 0)),
        ],
        out_specs=pl.BlockSpec((tm, D_MODEL // 2), lambda i, f, te, tv, nu: (i, 0)),
        scratch_shapes=[pltpu.VMEM((tm, D_MODEL), BF16), pltpu.VMEM((tm, D_MODEL), F32)],
    )
    return pl.pallas_call(
        _expert_kernel,
        grid_spec=grid_spec,
        out_shape=jax.ShapeDtypeStruct((n_rows, D_MODEL // 2), jnp.uint32),
        compiler_params=_cparams("arbitrary", "arbitrary"),
        name="expert_ffn",
    )(tile_expert, tile_valid, n_used, xs, wg, wu, wd)


def _combine_kernel(y0_ref, y1_ref, gt_ref, x_ref, mod_ref, modc_ref, g_ref, b_ref, o_ref, *, tm, row0):
    is_ctx = _row_is_ctx(tm, pl.program_id(1), row0)
    gt = gt_ref[0]
    ffn = gt[:, 0:1] * _unpack_bf16_pairs(y0_ref[0, 0]) + gt[:, 1:2] * _unpack_bf16_pairs(y1_ref[0, 0])
    g2 = _mod_pick(is_ctx, modc_ref, mod_ref, 5)
    o_ref[0] = _layernorm_rows(DEEPNORM_ALPHA * x_ref[0] + (1.0 + g2) * ffn, g_ref[...], b_ref[...])


def _moe_combine(yg, gates, x1, mod, ln_g, ln_b, *, tm, row0):
    b, rows, _ = x1.shape
    row_spec = lambda n: pl.BlockSpec((1, tm, n), lambda i, j: (i, j, 0))
    vec = lambda n: pl.BlockSpec((1, n), lambda i, j: (0, 0))
    return pl.pallas_call(
        functools.partial(_combine_kernel, tm=tm, row0=row0),
        grid=(b, rows // tm),
        in_specs=[
            pl.BlockSpec((1, 1, tm, D_MODEL // 2), lambda i, j: (0, i, j, 0)),
            pl.BlockSpec((1, 1, tm, D_MODEL // 2), lambda i, j: (1, i, j, 0)),
            row_spec(LANES), row_spec(D_MODEL),
            pl.BlockSpec((1, 1, 6 * D_MODEL), lambda i, j: (i, 0, 0)),
            pl.BlockSpec((1, 1, 6 * D_MODEL), lambda i, j: (MOD_ROWS - 1, 0, 0)),
            vec(D_MODEL), vec(D_MODEL),
        ],
        out_specs=row_spec(D_MODEL),
        out_shape=jax.ShapeDtypeStruct((b, rows, D_MODEL), F32),
        compiler_params=_cparams("arbitrary", "arbitrary"),
        name="moe_combine",
    )(yg, yg, gates, x1, mod, mod, ln_g.reshape(1, -1), ln_b.reshape(1, -1))


def _routing_plan(eidx, *, tm):
    t = eidx.shape[0]
    n_tiles = (t * TOP_K) // tm + N_EXPERTS
    flat = eidx.reshape(-1)
    onehot = (flat[:, None] == jnp.arange(N_EXPERTS, dtype=jnp.int32)[None, :]).astype(jnp.int32)
    csum = jnp.cumsum(onehot, axis=0)
    counts = csum[-1]
    rank = jnp.sum((csum - onehot) * onehot, axis=1)
    tiles = (counts + tm - 1) // tm
    tile_end = jnp.cumsum(tiles)
    tile_start = tile_end - tiles
    pos = jnp.sum(onehot * tile_start[None, :], axis=1) * tm + rank
    n_used = tile_end[-1]
    tile_ids = jnp.arange(n_tiles, dtype=jnp.int32)
    tile_expert = jnp.sum((jnp.minimum(tile_ids, n_used - 1)[:, None] >= tile_end[None, :]).astype(jnp.int32), axis=1)
    rows_left = counts[tile_expert] - (tile_ids - tile_start[tile_expert]) * tm
    tile_valid = jnp.where(tile_ids < n_used, jnp.clip(rows_left, 0, tm), 0)
    return (pos.astype(jnp.int32), n_tiles * tm, tile_expert.astype(jnp.int32),
            tile_valid.astype(jnp.int32), n_used.reshape(1).astype(jnp.int32))


def _moe_ffn(h2, eidx, gates, x1, mod, wg, wu, wd, ln_g, ln_b, *, first_expert, row0, tm_row, tm_expert, tf):
    b, rows, _ = x1.shape
    t = b * rows
    pos, n_sorted, tile_expert, tile_valid, n_used = _routing_plan(eidx.reshape(t, LANES)[:, 0:TOP_K], tm=tm_expert)
    pos_slots = pos.reshape(t, TOP_K).T
    xs = _scatter_rows(h2.reshape(t, D_MODEL // 2), pos_slots, n_sorted)
    y = _expert_ffn(xs, tile_expert + first_expert, tile_valid, n_used, wg, wu, wd, tm=tm_expert, tf=tf)
    yg = _gather_rows(y, pos_slots.reshape(-1)).reshape(TOP_K, b, rows, D_MODEL // 2)
    return _moe_combine(yg, gates, x1, mod, ln_g, ln_b, tm=tm_row, row0=row0)


TM_ROW = 768
TM_LAST = 256
TF_DENSE = 1408
TM_EXPERT = 1024
TF_EXPERT = 512
SC_CORES = 2
SC_SUBCORES = 16
GATHER_CHUNK = 64


def kernel(x, c, ctx, c_ctx, w_mod, b_mod, w_in, w_out, na_rpb, conv_w, conv_b, dt_bias, a_log, d_skip,
           ssd_norm_w, ln_g, ln_b, ffn_w_gate, ffn_w_up, ffn_w_down, router_w, moe_w_gate, moe_w_up,
           moe_w_down):
    b, seq, _ = x.shape
    rows_lat = seq // GRID_W
    xa = jnp.concatenate([ctx, x], axis=1)
    cvec = jnp.zeros((MOD_ROWS, D_MODEL), F32).at[0:b].set(c).at[MOD_ROWS - 1].set(c_ctx)
    mod_all = _modulation(cvec, w_mod, b_mod)
    win_base, win_cfg_id, win_cfgs = _attn_window_plan(rows_lat)
    cos_t, sin_t = _rope_tables(seq)
    sel = _head_select_matrix()
    w_in_p = jnp.pad(w_in, ((0, 0), (0, 0), (0, D_IN_PAD - D_IN_PROJ))).astype(BF16)
    w_out16 = w_out.astype(BF16)

    for layer in range(DEPTH):
        last = layer == DEPTH - 1
        is_moe = layer % 2 == 1
        mod = mod_all[layer].reshape(MOD_ROWS, 1, 6 * D_MODEL)
        qkv, z, xbc, dt = _in_proj(xa, mod, w_in_p[layer], tm=TM_ROW)
        a = _attention(qkv, _expand_rel_bias(na_rpb[layer], win_cfgs), jnp.asarray(win_base),
                       jnp.asarray(win_cfg_id), with_ctx_out=not last)
        y = _ssd_scan(xbc, dt, conv_w[layer], conv_b[layer], dt_bias[layer], a_log[layer], d_skip[layer],
                      cos_t, sin_t, sel)
        router_p = None
        if is_moe:
            r_pad = jnp.pad(router_w[layer // 2], ((0, 0), (0, LANES - N_EXPERTS)))
            r_hi = r_pad.astype(BF16)
            r_mid = (r_pad - r_hi.astype(F32)).astype(BF16)
            router_p = jnp.concatenate([r_hi, r_mid, r_hi], axis=0)
        row0, tm_post = (CTX_LEN, TM_LAST) if last else (0, TM_ROW)
        outs = _out_proj(a, y, z, xa, mod, w_out16[layer], ssd_norm_w[layer], ln_g[layer, 0], ln_b[layer, 0],
                         router_p, tm=tm_post, row0=row0)
        if is_moe:
            x1, h2, eidx, gates = outs
            m = layer // 2
            xa = _moe_ffn(h2, eidx, gates, x1, mod, moe_w_gate.reshape(-1, *moe_w_gate.shape[2:]),
                          moe_w_up.reshape(-1, *moe_w_up.shape[2:]), moe_w_down.reshape(-1, *moe_w_down.shape[2:]),
                          ln_g[layer, 1], ln_b[layer, 1], first_expert=m * N_EXPERTS, row0=row0,
                          tm_row=tm_post, tm_expert=TM_EXPERT, tf=TF_EXPERT)
        else:
            assert not last
            x1, h2 = outs
            m = layer // 2
            xa = _dense_ffn(h2, x1, mod, ffn_w_gate[m].astype(BF16), ffn_w_up[m].astype(BF16),
                            ffn_w_down[m].astype(BF16), ln_g[layer, 1], ln_b[layer, 1], tm=TM_ROW, tf=TF_DENSE)
    return xa
```

```python
import functools
import math

import numpy as np
import jax
import jax.numpy as jnp
from jax import lax
from jax.experimental import pallas as pl
from jax.experimental.pallas import tpu as pltpu
from jax.experimental.pallas import tpu_sc as plsc

F32 = jnp.float32
BF16 = jnp.bfloat16
HIGHEST = lax.Precision.HIGHEST

D_MODEL = 1024
DEPTH = 4
GRID_W = 64
CTX_LEN = 256
N_HEADS_NA = 8
HEAD_DIM_NA = 64
D_NA = N_HEADS_NA * HEAD_DIM_NA
WIN_R = 8
WIN_C = 16
N_HEADS_SSD = 8
HEAD_DIM_SSD = 64
D_SSD = N_HEADS_SSD * HEAD_DIM_SSD
N_GROUPS_SSD = 2
D_STATE = 128
D_CONV = 5
CHUNK = 128
ROPE_BASE = 10000.0
GN = N_GROUPS_SSD * D_STATE
CONV_CH = D_SSD + 2 * GN
D_IN_PROJ = 3 * D_NA + D_SSD + CONV_CH + 2 * N_HEADS_SSD
N_EXPERTS = 8
TOP_K = 2
DEEPNORM_ALPHA = (2.0 * DEPTH) ** 0.25
LN_EPS = 1e-5
RMS_EPS = 1e-5

LANES = 128
SUBLANES = 8
VMEM_LIMIT = 56 * 1024 * 1024
DT_PAD = LANES
D_IN_PAD = D_IN_PROJ - 2 * N_HEADS_SSD + DT_PAD
OFF_Z = 3 * D_NA
OFF_XBC = OFF_Z + D_SSD
OFF_DT = OFF_XBC + CONV_CH
MOD_ROWS = 24
HEAD_PAIR = 2 * HEAD_DIM_NA
ROWS_PER_STEP = 2
WIN_ROWS = WIN_R + ROWS_PER_STEP - 1


def _cparams(*sem):
    return pltpu.CompilerParams(dimension_semantics=sem, vmem_limit_bytes=VMEM_LIMIT)


def _silu(v):
    return v * jax.nn.sigmoid(v)


def _layernorm_rows(v, g, b):
    mu = jnp.mean(v, axis=-1, keepdims=True)
    d = v - mu
    var = jnp.mean(d * d, axis=-1, keepdims=True)
    return d * lax.rsqrt(var + LN_EPS) * g + b


def _pack_bf16_pairs(v):
    n = v.shape[1] // 2
    lo = pltpu.bitcast(v[:, 0:n].astype(BF16).astype(F32), jnp.uint32)
    hi = pltpu.bitcast(v[:, n:2 * n].astype(BF16).astype(F32), jnp.uint32)
    return hi | (lo >> 16)


def _unpack_bf16_pairs(w):
    lo = pltpu.bitcast(w << 16, F32)
    hi = pltpu.bitcast(w & jnp.uint32(0xFFFF0000), F32)
    return jnp.concatenate([lo, hi], axis=1)


def _row_is_ctx(tm, j, row0=0):
    row = lax.broadcasted_iota(jnp.int32, (tm, 1), 0) + j * tm + row0
    return row < CTX_LEN


def _mod_pick(is_ctx, modc_ref, mod_ref, seg):
    lo, hi = seg * D_MODEL, (seg + 1) * D_MODEL
    return jnp.where(is_ctx, modc_ref[0, :, lo:hi], mod_ref[0, :, lo:hi])


def _mod_kernel(c_ref, w_ref, b_ref, o_ref):
    cs = _silu(c_ref[...])
    o_ref[0] = jnp.dot(cs, w_ref[0], preferred_element_type=F32, precision=HIGHEST) + b_ref[0]


def _modulation(cvec, w_mod, b_mod):
    tn = D_MODEL
    return pl.pallas_call(
        _mod_kernel,
        grid=(DEPTH, 6 * D_MODEL // tn),
        in_specs=[
            pl.BlockSpec((MOD_ROWS, D_MODEL), lambda l, n: (0, 0)),
            pl.BlockSpec((1, D_MODEL, tn), lambda l, n: (l, 0, n)),
            pl.BlockSpec((1, 1, tn), lambda l, n: (l, 0, n)),
        ],
        out_specs=pl.BlockSpec((1, MOD_ROWS, tn), lambda l, n: (l, 0, n)),
        out_shape=jax.ShapeDtypeStruct((DEPTH, MOD_ROWS, 6 * D_MODEL), F32),
        compiler_params=_cparams("arbitrary", "arbitrary"),
        name="modulation",
    )(cvec, w_mod, b_mod.reshape(DEPTH, 1, 6 * D_MODEL))


def _in_proj_kernel(x_ref, mod_ref, modc_ref, w_ref, qkv_ref, z_ref, xbc_ref, dt_ref, *, tm):
    is_ctx = _row_is_ctx(tm, pl.program_id(1))
    sh = _mod_pick(is_ctx, modc_ref, mod_ref, 0)
    sc = _mod_pick(is_ctx, modc_ref, mod_ref, 1)
    h = (x_ref[0] * (1.0 + sc) + sh).astype(BF16)
    qkv_ref[0] = jnp.dot(h, w_ref[:, 0:OFF_Z], preferred_element_type=F32).astype(BF16)
    z_ref[0] = jnp.dot(h, w_ref[:, OFF_Z:OFF_XBC], preferred_element_type=F32)
    xbc_ref[0] = jnp.dot(h, w_ref[:, OFF_XBC:OFF_DT], preferred_element_type=F32)
    dt_ref[0] = jnp.dot(h, w_ref[:, OFF_DT:D_IN_PAD], preferred_element_type=F32)


def _in_proj(xa, mod, w_in_p, *, tm):
    b, rows, _ = xa.shape
    row_spec = lambda n: pl.BlockSpec((1, tm, n), lambda i, j: (i, j, 0))
    return pl.pallas_call(
        functools.partial(_in_proj_kernel, tm=tm),
        grid=(b, rows // tm),
        in_specs=[
            row_spec(D_MODEL),
            pl.BlockSpec((1, 1, 6 * D_MODEL), lambda i, j: (i, 0, 0)),
            pl.BlockSpec((1, 1, 6 * D_MODEL), lambda i, j: (MOD_ROWS - 1, 0, 0)),
            pl.BlockSpec((D_MODEL, D_IN_PAD), lambda i, j: (0, 0)),
        ],
        out_specs=[row_spec(OFF_Z), row_spec(D_SSD), row_spec(CONV_CH), row_spec(DT_PAD)],
        out_shape=[
            jax.ShapeDtypeStruct((b, rows, OFF_Z), BF16),
            jax.ShapeDtypeStruct((b, rows, D_SSD), F32),
            jax.ShapeDtypeStruct((b, rows, CONV_CH), F32),
            jax.ShapeDtypeStruct((b, rows, DT_PAD), F32),
        ],
        compiler_params=_cparams("arbitrary", "arbitrary"),
        name="in_proj",
    )(xa, mod, mod, w_in_p)


def _attn_head_pair(q_pair, keys_vals, biases):
    m_rows = q_pair.shape[0]
    lane = lax.broadcasted_iota(jnp.int32, q_pair.shape, 1)
    zero = jnp.zeros_like(q_pair)
    q_stack = jnp.concatenate([jnp.where(lane < HEAD_DIM_NA, q_pair, zero),
                               jnp.where(lane >= HEAD_DIM_NA, q_pair, zero)], axis=0)
    scores = []
    for (k, _), bias in zip(keys_vals, biases):
        s = lax.dot_general(q_stack, k, (((1,), (1,)), ((), ())), preferred_element_type=F32)
        scores.append(s if bias is None else s + bias)
    m = scores[0].max(axis=-1, keepdims=True)
    for s in scores[1:]:
        m = jnp.maximum(m, s.max(axis=-1, keepdims=True))
    denom = jnp.zeros_like(m)
    acc = jnp.zeros((2 * m_rows, HEAD_PAIR), F32)
    for s, (_, v) in zip(scores, keys_vals):
        p = jnp.exp(s - m)
        denom = denom + p.sum(axis=-1, keepdims=True)
        acc = acc + jnp.dot(p.astype(BF16), v, preferred_element_type=F32)
    out = acc / denom
    return jnp.where(lane < HEAD_DIM_NA, out[0:m_rows], out[m_rows:2 * m_rows])


def _attn_kernel(base_ref, cfg_ref, qkv_ref, rpb_ref, o_ref, bias_ref, *, rows_lat, n_cfg, with_ctx_out):
    scale = HEAD_DIM_NA ** -0.5
    k_off, v_off = D_NA, 2 * D_NA
    n_pairs = N_HEADS_NA // 2
    q_rows = ROWS_PER_STEP * GRID_W
    win = WIN_ROWS * GRID_W

    @pl.when(pl.program_id(0) == 0)
    def _():
        qc = lax.broadcasted_iota(jnp.int32, (GRID_W, GRID_W), 0)
        kc = lax.broadcasted_iota(jnp.int32, (GRID_W, GRID_W), 1)
        col_start = jnp.clip(qc - WIN_C // 2, 0, GRID_W - WIN_C)
        in_win = (kc >= col_start) & (kc < col_start + WIN_C)

        def expand(idx, carry):
            cfg, p = idx // n_pairs, idx % n_pairs
            for hh in range(2):
                for s in range(ROWS_PER_STEP):
                    for w in range(WIN_ROWS):
                        r = (((cfg * N_HEADS_NA + 2 * p + hh) * ROWS_PER_STEP + s) * WIN_ROWS + w)
                        v = jnp.broadcast_to(rpb_ref[pl.ds(r, 1), :], (GRID_W, LANES))
                        blk = pltpu.roll(v, LANES - (WIN_C - 1), 1, stride=1, stride_axis=0)[:, 0:GRID_W]
                        q0 = (hh * ROWS_PER_STEP + s) * GRID_W
                        bias_ref[cfg, p, q0:q0 + GRID_W, w * GRID_W:(w + 1) * GRID_W] = jnp.where(
                            in_win, blk, -jnp.inf)
            return carry

        lax.fori_loop(0, n_cfg * n_pairs, expand, 0)

    def lat_rows(j, carry):
        q_base = pl.multiple_of(CTX_LEN + j * q_rows, q_rows)
        k_base = pl.multiple_of(CTX_LEN + base_ref[j] * GRID_W, GRID_W)
        cfg = cfg_ref[j]
        for p in range(n_pairs):
            lo = p * HEAD_PAIR
            q = qkv_ref[0, pl.ds(q_base, q_rows), lo:lo + HEAD_PAIR] * scale
            k_loc = qkv_ref[0, pl.ds(k_base, win), k_off + lo:k_off + lo + HEAD_PAIR]
            v_loc = qkv_ref[0, pl.ds(k_base, win), v_off + lo:v_off + lo + HEAD_PAIR]
            k_ctx = qkv_ref[0, 0:CTX_LEN, k_off + lo:k_off + lo + HEAD_PAIR]
            v_ctx = qkv_ref[0, 0:CTX_LEN, v_off + lo:v_off + lo + HEAD_PAIR]
            o = _attn_head_pair(q, [(k_loc, v_loc), (k_ctx, v_ctx)], [bias_ref[cfg, p], None])
            o_ref[0, pl.ds(q_base, q_rows), lo:lo + HEAD_PAIR] = o.astype(o_ref.dtype)
        return carry

    lax.fori_loop(0, rows_lat // ROWS_PER_STEP, lat_rows, 0)

    if with_ctx_out:
        for p in range(n_pairs):
            lo = p * HEAD_PAIR
            q = qkv_ref[0, 0:CTX_LEN, lo:lo + HEAD_PAIR] * scale
            k_ctx = qkv_ref[0, 0:CTX_LEN, k_off + lo:k_off + lo + HEAD_PAIR]
            v_ctx = qkv_ref[0, 0:CTX_LEN, v_off + lo:v_off + lo + HEAD_PAIR]
            o = _attn_head_pair(q, [(k_ctx, v_ctx)], [None])
            o_ref[0, 0:CTX_LEN, lo:lo + HEAD_PAIR] = o.astype(o_ref.dtype)
    else:
        o_ref[0, 0:CTX_LEN, :] = jnp.zeros((CTX_LEN, D_NA), o_ref.dtype)


def _attention(qkv, rpb_rows, win_base, cfg, *, n_cfg, with_ctx_out):
    b, rows, _ = qkv.shape
    rows_lat = (rows - CTX_LEN) // GRID_W
    grid_spec = pltpu.PrefetchScalarGridSpec(
        num_scalar_prefetch=2,
        grid=(b,),
        in_specs=[
            pl.BlockSpec((1, rows, OFF_Z), lambda i, wb, cf: (i, 0, 0)),
            pl.BlockSpec(rpb_rows.shape, lambda i, wb, cf: (0, 0)),
        ],
        out_specs=pl.BlockSpec((1, rows, D_NA), lambda i, wb, cf: (i, 0, 0)),
        scratch_shapes=[pltpu.VMEM((n_cfg, N_HEADS_NA // 2, 2 * ROWS_PER_STEP * GRID_W, WIN_ROWS * GRID_W), F32)],
    )
    return pl.pallas_call(
        functools.partial(_attn_kernel, rows_lat=rows_lat, n_cfg=n_cfg, with_ctx_out=with_ctx_out),
        grid_spec=grid_spec,
        out_shape=jax.ShapeDtypeStruct((b, rows, D_NA), BF16),
        compiler_params=_cparams("arbitrary"),
        name="attention",
    )(win_base, cfg, qkv, rpb_rows)


def _attn_window_plan(rows_lat):
    kr = min(WIN_R, rows_lat)
    assert rows_lat % ROWS_PER_STEP == 0 and rows_lat >= WIN_ROWS and kr == WIN_R
    bases, cfg_ids, configs = [], [], []
    for i in range(0, rows_lat, ROWS_PER_STEP):
        r0 = [int(np.clip(i + s - kr // 2, 0, rows_lat - kr)) for s in range(ROWS_PER_STEP)]
        base = min(r0[0], rows_lat - WIN_ROWS)
        assert all(0 <= r - base and r - base + kr <= WIN_ROWS for r in r0)
        c = tuple((i + s - r0[s], r0[s] - base) for s in range(ROWS_PER_STEP))
        if c not in configs:
            configs.append(c)
        bases.append(base)
        cfg_ids.append(configs.index(c))
    return np.asarray(bases, np.int32), np.asarray(cfg_ids, np.int32), configs


def _rel_bias_rows(rpb, configs):
    d = np.asarray([[c[s][0] for s in range(ROWS_PER_STEP)] for c in configs])[:, :, None]
    o = np.asarray([[c[s][1] for s in range(ROWS_PER_STEP)] for c in configs])[:, :, None]
    r = np.arange(WIN_ROWS)[None, None, :] - o
    row_ok = (r >= 0) & (r < WIN_R)
    row_idx = np.clip(r - d + WIN_R - 1, 0, 2 * WIN_R - 2)
    t = rpb.astype(F32)[:, row_idx]
    t = jnp.pad(t, ((0, 0),) * 4 + ((0, LANES - (2 * WIN_C - 1)),))
    t = jnp.where(jnp.asarray(row_ok)[None, :, :, :, None], t, -jnp.inf)
    return t.transpose(1, 0, 2, 3, 4).reshape(-1, LANES)


def _split3(v):
    hi = v.astype(BF16)
    rest = v - hi.astype(F32)
    mid = rest.astype(BF16)
    lo = (rest - mid.astype(F32)).astype(BF16)
    return hi, mid, lo


def _ssd_kernel(xbc_ref, dt_ref, convw_ref, convb_ref, dtb_ref, alog_ref, dskip_ref, cos_ref, sin_ref,
                sel_ref, y_ref, htf_ref, htb_ref, stf_ref, c_ref, ef_ref, ext_ref, *, n_chunks):
    n_ctx_chunks = CTX_LEN // CHUNK
    hp = D_SSD // N_GROUPS_SSD
    htf_ref[...] = jnp.zeros_like(htf_ref)
    htb_ref[...] = jnp.zeros_like(htb_ref)

    li = lax.broadcasted_iota(jnp.int32, (CHUNK, CHUNK), 0)
    si = lax.broadcasted_iota(jnp.int32, (CHUNK, CHUNK), 1)
    lower = li >= si
    upper = li <= si
    tri = jnp.concatenate([lower, upper], axis=0).astype(F32).astype(BF16)
    lane_c = lax.broadcasted_iota(jnp.int32, (CHUNK, LANES), 1)
    rope_lo = (lane_c % (D_STATE // 2)) < (D_STATE // 4)
    pair_lane = lax.broadcasted_iota(jnp.int32, (CHUNK, HEAD_PAIR), 1)

    def rope(w, cos, sin):
        partner = jnp.where(rope_lo, pltpu.roll(w, LANES - D_STATE // 4, 1), pltpu.roll(w, D_STATE // 4, 1))
        return w * cos + partner * sin

    def local_pass(k, carry):
        c = jnp.where(k < n_ctx_chunks, n_ctx_chunks - 1 - k, n_chunks + n_ctx_chunks - 1 - k)
        base = pl.multiple_of(c * CHUNK, CHUNK)
        is_start = (c == 0) | (c == n_ctx_chunks)
        is_end = (c == n_ctx_chunks - 1) | (c == n_chunks - 1)
        prev_base = pl.multiple_of(jnp.maximum(base - SUBLANES, 0), SUBLANES)
        next_base = pl.multiple_of(jnp.minimum(base + CHUNK, (n_chunks - 1) * CHUNK + CHUNK - SUBLANES), SUBLANES)
        ext_ref[0:SUBLANES, :] = jnp.where(is_start, 0.0, xbc_ref[0, pl.ds(prev_base, SUBLANES), :])
        ext_ref[SUBLANES:SUBLANES + CHUNK, :] = xbc_ref[0, pl.ds(base, CHUNK), :]
        ext_ref[SUBLANES + CHUNK:2 * SUBLANES + CHUNK, :] = jnp.where(
            is_end, 0.0, xbc_ref[0, pl.ds(next_base, SUBLANES), :])
        ext = ext_ref[...]
        conv = convb_ref[...]
        for t in range(D_CONV):
            shift = (D_CONV // 2 - t) % ext.shape[0]
            tap = ext if shift == 0 else pltpu.roll(ext, shift, 0)
            conv = conv + tap[SUBLANES:SUBLANES + CHUNK] * convw_ref[t:t + 1, :]
        u = _silu(conv)
        xs = u[:, 0:D_SSD]
        cos = cos_ref[pl.ds(base, CHUNK), :]
        sin = sin_ref[pl.ds(base, CHUNK), :]
        bm = [rope(u[:, D_SSD + g * D_STATE:D_SSD + (g + 1) * D_STATE], cos, sin) for g in range(N_GROUPS_SSD)]
        cm = [rope(u[:, D_SSD + GN + g * D_STATE:D_SSD + GN + (g + 1) * D_STATE], cos, sin)
              for g in range(N_GROUPS_SSD)]

        dt = jax.nn.softplus(dt_ref[0, pl.ds(base, CHUNK), :] + dtb_ref[...])
        a = dt * (-jnp.exp(alog_ref[...]))
        sums = jnp.dot(tri, jnp.concatenate(_split3(a), axis=1), preferred_element_type=F32)
        sums = sums[:, 0:LANES] + sums[:, LANES:2 * LANES] + sums[:, 2 * LANES:3 * LANES]
        acs = jnp.where(lane_c < N_HEADS_SSD, sums[0:CHUNK], sums[CHUNK:2 * CHUNK])
        acs_t = acs.T
        dt_t = dt.T
        wide = jnp.dot(jnp.concatenate(_split3(jnp.concatenate([acs, dt], axis=0)), axis=1), sel_ref[...],
                       preferred_element_type=F32)
        acs_x, dt_x = wide[0:CHUNK], wide[CHUNK:2 * CHUNK]
        arg_f, arg_b = acs_x[:, 0:D_SSD], acs_x[:, D_SSD:2 * D_SSD]
        e_f = jnp.exp(arg_f)
        e_b = jnp.exp(arg_b)
        x_f = (xs * dt_x[:, 0:D_SSD] * jnp.exp(arg_f[CHUNK - 1:CHUNK, :] - arg_f)).astype(BF16)
        x_b = (xs * dt_x[:, D_SSD:2 * D_SSD] * jnp.exp(arg_b[0:1, :] - arg_b)).astype(BF16)

        xs16 = xs.astype(BF16)
        y_parts = []
        for g in range(N_GROUPS_SSD):
            bt = bm[g].T.astype(BF16)
            c16 = cm[g].astype(BF16)
            gmat = jnp.dot(c16, bt, preferred_element_type=F32)
            gs = slice(g * hp, (g + 1) * hp)
            heads = range(g * (N_HEADS_SSD // N_GROUPS_SSD), (g + 1) * (N_HEADS_SSD // N_GROUPS_SSD))
            ms = []
            for h in heads:
                jf, jb = h, N_HEADS_SSD + h
                wf = jnp.where(lower, jnp.exp(acs[:, jf:jf + 1] - acs_t[jf:jf + 1, :]), 0.0) * dt_t[jf:jf + 1, :]
                wb = jnp.where(upper, jnp.exp(acs[:, jb:jb + 1] - acs_t[jb:jb + 1, :]), 0.0) * dt_t[jb:jb + 1, :]
                ms.append((gmat * (wf + wb)).astype(BF16))
            y_g = []
            for q in range(len(ms) // 2):
                lo = g * hp + q * HEAD_PAIR
                xp = xs16[:, lo:lo + HEAD_PAIR]
                rhs = jnp.concatenate([jnp.where(pair_lane < HEAD_DIM_SSD, xp, jnp.zeros_like(xp)),
                                       jnp.where(pair_lane >= HEAD_DIM_SSD, xp, jnp.zeros_like(xp))], axis=0)
                lhs = jnp.concatenate([ms[2 * q], ms[2 * q + 1]], axis=1)
                y_g.append(jnp.dot(lhs, rhs, preferred_element_type=F32))
            y_diag = jnp.concatenate(y_g, axis=1)
            y_off = jnp.dot(c16, htb_ref[g].astype(BF16), preferred_element_type=F32) * e_b[:, gs]
            y_parts.append(y_diag + y_off)
            htb_ref[g] = htb_ref[g] * e_b[0:1, gs] + jnp.dot(bt, x_b[:, gs], preferred_element_type=F32)
            stf_ref[c, g] = jnp.dot(bt, x_f[:, gs], preferred_element_type=F32)
            c_ref[c, g] = c16
        ef_ref[c] = e_f
        y_ref[0, pl.ds(base, CHUNK), :] = jnp.concatenate(y_parts, axis=1) + dskip_ref[...] * xs
        return carry

    lax.fori_loop(0, n_chunks, local_pass, 0)

    def forward_pass(c, carry):
        base = pl.multiple_of(c * CHUNK, CHUNK)
        e_f = ef_ref[c]
        y_parts = []
        for g in range(N_GROUPS_SSD):
            gs = slice(g * hp, (g + 1) * hp)
            y_parts.append(jnp.dot(c_ref[c, g], htf_ref[g].astype(BF16), preferred_element_type=F32) * e_f[:, gs])
            htf_ref[g] = htf_ref[g] * e_f[CHUNK - 1:CHUNK, gs] + stf_ref[c, g]
        y_ref[0, pl.ds(base, CHUNK), :] = y_ref[0, pl.ds(base, CHUNK), :] + jnp.concatenate(y_parts, axis=1)
        return carry

    lax.fori_loop(0, n_chunks, forward_pass, 0)


def _ssd_scan(xbc, dt, conv_w, conv_b, dt_bias, a_log, d_skip, cos_t, sin_t, sel):
    b, rows, _ = xbc.shape
    n_chunks = rows // CHUNK
    hp = D_SSD // N_GROUPS_SSD
    full = lambda a: pl.BlockSpec(a.shape, lambda i: (0,) * a.ndim)
    pad16 = lambda v: jnp.pad(v.reshape(1, -1), ((0, 0), (0, DT_PAD - v.size)))
    conv_w8 = jnp.pad(conv_w, ((0, SUBLANES - D_CONV), (0, 0)))
    params = [conv_w8, conv_b.reshape(1, -1), pad16(dt_bias), pad16(a_log),
              jnp.repeat(d_skip, HEAD_DIM_SSD).reshape(1, -1), cos_t, sin_t, sel]
    return pl.pallas_call(
        functools.partial(_ssd_kernel, n_chunks=n_chunks),
        grid=(b,),
        in_specs=[pl.BlockSpec((1, rows, CONV_CH), lambda i: (i, 0, 0)),
                  pl.BlockSpec((1, rows, DT_PAD), lambda i: (i, 0, 0))] + [full(p) for p in params],
        out_specs=pl.BlockSpec((1, rows, D_SSD), lambda i: (i, 0, 0)),
        out_shape=jax.ShapeDtypeStruct((b, rows, D_SSD), F32),
        scratch_shapes=[
            pltpu.VMEM((N_GROUPS_SSD, D_STATE, hp), F32),
            pltpu.VMEM((N_GROUPS_SSD, D_STATE, hp), F32),
            pltpu.VMEM((n_chunks, N_GROUPS_SSD, D_STATE, hp), F32),
            pltpu.VMEM((n_chunks, N_GROUPS_SSD, CHUNK, D_STATE), BF16),
            pltpu.VMEM((n_chunks, CHUNK, D_SSD), F32),
            pltpu.VMEM((CHUNK + 2 * SUBLANES, CONV_CH), F32),
        ],
        compiler_params=_cparams("arbitrary"),
        name="ssd_scan",
    )(xbc, dt, *params)


def _rope_tables(seq):
    half, quarter = D_STATE // 2, D_STATE // 4
    t = jnp.arange(seq)
    inv_freq = ROPE_BASE ** (-jnp.arange(quarter, dtype=F32) / quarter)
    ang_r = (t // GRID_W).astype(F32)[:, None] * inv_freq
    ang_c = (t % GRID_W).astype(F32)[:, None] * inv_freq
    cos = jnp.concatenate([jnp.cos(ang_r)] * 2 + [jnp.cos(ang_c)] * 2, axis=-1)
    sin = jnp.concatenate([-jnp.sin(ang_r), jnp.sin(ang_r), -jnp.sin(ang_c), jnp.sin(ang_c)], axis=-1)
    cos = jnp.concatenate([jnp.ones((CTX_LEN, 2 * half), F32), cos], axis=0)
    sin = jnp.concatenate([jnp.zeros((CTX_LEN, 2 * half), F32), sin], axis=0)
    return cos, sin


def _head_select_matrix():
    sel = np.zeros((DT_PAD, 2 * D_SSD), np.float32)
    for j in range(2 * N_HEADS_SSD):
        sel[j, j * HEAD_DIM_SSD:(j + 1) * HEAD_DIM_SSD] = 1.0
    return jnp.asarray(np.concatenate([sel] * 3, axis=0), dtype=BF16)


def _out_proj_kernel(*refs, tm, with_router, row0):
    if with_router:
        (a_ref, y_ref, z_ref, x_ref, mod_ref, modc_ref, w_ref, nw_ref, g_ref, b_ref, r_ref,
         x1_ref, h2_ref, ei_ref, gt_ref) = refs
    else:
        a_ref, y_ref, z_ref, x_ref, mod_ref, modc_ref, w_ref, nw_ref, g_ref, b_ref, x1_ref, h2_ref = refs
    is_ctx = _row_is_ctx(tm, pl.program_id(1), row0)
    yg = y_ref[0] * _silu(z_ref[0])
    gw = D_SSD // N_GROUPS_SSD
    parts = []
    for g in range(N_GROUPS_SSD):
        part = yg[:, g * gw:(g + 1) * gw]
        parts.append(part * lax.rsqrt(jnp.mean(part * part, axis=-1, keepdims=True) + RMS_EPS))
    s = (jnp.concatenate(parts, axis=1) * nw_ref[...]).astype(BF16)
    mix = (jnp.dot(a_ref[0], w_ref[0:D_NA, :], preferred_element_type=F32)
           + jnp.dot(s, w_ref[D_NA:D_NA + D_SSD, :], preferred_element_type=F32))
    g1 = _mod_pick(is_ctx, modc_ref, mod_ref, 2)
    x1 = _layernorm_rows(DEEPNORM_ALPHA * x_ref[0] + (1.0 + g1) * mix, g_ref[...], b_ref[...])
    x1_ref[0] = x1
    sh2 = _mod_pick(is_ctx, modc_ref, mod_ref, 3)
    sc2 = _mod_pick(is_ctx, modc_ref, mod_ref, 4)
    h2 = x1 * (1.0 + sc2) + sh2
    if with_router:
        h2_ref[0] = _pack_bf16_pairs(h2)
    else:
        h2_ref[0] = h2.astype(h2_ref.dtype)
    if with_router:
        h_hi = h2.astype(BF16)
        h_mid = (h2 - h_hi.astype(F32)).astype(BF16)
        logits = jnp.dot(jnp.concatenate([h_hi, h_hi, h_mid], axis=1), r_ref[...],
                         preferred_element_type=F32)
        lane = lax.broadcasted_iota(jnp.int32, logits.shape, 1)
        logits = jnp.where(lane < N_EXPERTS, logits, -jnp.inf)
        m1 = logits.max(axis=-1, keepdims=True)
        i1 = jnp.where(logits == m1, lane, LANES).min(axis=-1, keepdims=True)
        rest = jnp.where(lane == i1, -jnp.inf, logits)
        m2 = rest.max(axis=-1, keepdims=True)
        i2 = jnp.where(rest == m2, lane, LANES).min(axis=-1, keepdims=True)
        e2 = jnp.exp(m2 - m1)
        gate1 = 1.0 / (1.0 + e2)
        gate2 = e2 / (1.0 + e2)
        ei_ref[0] = jnp.where(lane == 0, i1, jnp.where(lane == 1, i2, 0))
        gt_ref[0] = jnp.where(lane == 0, gate1, jnp.where(lane == 1, gate2, 0.0))


def _out_proj(a, y, z, xa, mod, w_out, norm_w, ln_g, ln_b, router_p, *, tm, row0=0):
    b, rows_in, _ = xa.shape
    rows = rows_in - row0
    assert row0 % tm == 0 and rows % tm == 0
    with_router = router_p is not None
    row_spec = lambda n: pl.BlockSpec((1, tm, n), lambda i, j: (i, j, 0))
    in_row_spec = lambda n: pl.BlockSpec((1, tm, n), lambda i, j: (i, j + row0 // tm, 0))
    vec = lambda n: pl.BlockSpec((1, n), lambda i, j: (0, 0))
    in_specs = [
        in_row_spec(D_NA), in_row_spec(D_SSD), in_row_spec(D_SSD), in_row_spec(D_MODEL),
        pl.BlockSpec((1, 1, 6 * D_MODEL), lambda i, j: (i, 0, 0)),
        pl.BlockSpec((1, 1, 6 * D_MODEL), lambda i, j: (MOD_ROWS - 1, 0, 0)),
        pl.BlockSpec((D_NA + D_SSD, D_MODEL), lambda i, j: (0, 0)),
        vec(D_SSD), vec(D_MODEL), vec(D_MODEL),
    ]
    args = [a, y, z, xa, mod, mod, w_out, norm_w.reshape(1, -1), ln_g.reshape(1, -1), ln_b.reshape(1, -1)]
    h2_cols, h2_dtype = (D_MODEL // 2, jnp.uint32) if with_router else (D_MODEL, BF16)
    out_specs = [row_spec(D_MODEL), row_spec(h2_cols)]
    out_shape = [jax.ShapeDtypeStruct((b, rows, D_MODEL), F32), jax.ShapeDtypeStruct((b, rows, h2_cols), h2_dtype)]
    if with_router:
        in_specs.append(pl.BlockSpec((3 * D_MODEL, LANES), lambda i, j: (0, 0)))
        args.append(router_p)
        out_specs += [row_spec(LANES), row_spec(LANES)]
        out_shape += [jax.ShapeDtypeStruct((b, rows, LANES), jnp.int32), jax.ShapeDtypeStruct((b, rows, LANES), F32)]
    return pl.pallas_call(
        functools.partial(_out_proj_kernel, tm=tm, with_router=with_router, row0=row0),
        grid=(b, rows // tm),
        in_specs=in_specs,
        out_specs=out_specs,
        out_shape=out_shape,
        compiler_params=_cparams("arbitrary", "arbitrary"),
        name="out_proj",
    )(*args)


def _ffn_kernel(h_ref, x_ref, mod_ref, modc_ref, wg_ref, wu_ref, wd_ref, g_ref, b_ref, o_ref, *, tm, tf):
    is_ctx = _row_is_ctx(tm, pl.program_id(1))
    h = h_ref[0]
    d_ff = wg_ref.shape[1]
    acc = jnp.zeros((tm, D_MODEL), F32)
    for f in range(d_ff // tf):
        fs = slice(f * tf, (f + 1) * tf)
        gate = jnp.dot(h, wg_ref[:, fs], preferred_element_type=F32)
        up = jnp.dot(h, wu_ref[:, fs], preferred_element_type=F32)
        acc = acc + jnp.dot((_silu(gate) * up).astype(BF16), wd_ref[fs, :], preferred_element_type=F32)
    g2 = _mod_pick(is_ctx, modc_ref, mod_ref, 5)
    o_ref[0] = _layernorm_rows(DEEPNORM_ALPHA * x_ref[0] + (1.0 + g2) * acc, g_ref[...], b_ref[...])


def _dense_ffn(h2, x1, mod, wg, wu, wd, ln_g, ln_b, *, tm, tf):
    b, rows, _ = x1.shape
    d_ff = wg.shape[1]
    row_spec = lambda n: pl.BlockSpec((1, tm, n), lambda i, j: (i, j, 0))
    vec = lambda n: pl.BlockSpec((1, n), lambda i, j: (0, 0))
    const = lambda s: pl.BlockSpec(s, lambda i, j: (0, 0), pipeline_mode=pl.Buffered(1))
    return pl.pallas_call(
        functools.partial(_ffn_kernel, tm=tm, tf=tf),
        grid=(b, rows // tm),
        in_specs=[
            row_spec(D_MODEL), row_spec(D_MODEL),
            pl.BlockSpec((1, 1, 6 * D_MODEL), lambda i, j: (i, 0, 0)),
            pl.BlockSpec((1, 1, 6 * D_MODEL), lambda i, j: (MOD_ROWS - 1, 0, 0)),
            const((D_MODEL, d_ff)), const((D_MODEL, d_ff)), const((d_ff, D_MODEL)),
            vec(D_MODEL), vec(D_MODEL),
        ],
        out_specs=row_spec(D_MODEL),
        out_shape=jax.ShapeDtypeStruct((b, rows, D_MODEL), F32),
        compiler_params=_cparams("arbitrary", "arbitrary"),
        name="dense_ffn",
    )(h2, x1, mod, mod, wg, wu, wd, ln_g.reshape(1, -1), ln_b.reshape(1, -1))


def _gather_rows(table, idx):
    n, d = idx.shape[0], table.shape[1]
    workers = SC_CORES * SC_SUBCORES
    n_chunks = n // (workers * GATHER_CHUNK)
    assert n == workers * n_chunks * GATHER_CHUNK
    per_worker = n_chunks * GATHER_CHUNK
    mesh = plsc.VectorSubcoreMesh(core_axis_name="c", subcore_axis_name="s")

    assert n_chunks % 2 == 0

    def body(table_hbm, idx_hbm, out_hbm, idx_v, rows_v, sems):
        wid = lax.axis_index("s") * SC_CORES + lax.axis_index("c")
        pltpu.sync_copy(idx_hbm.at[wid], idx_v)
        base = wid * per_worker

        def gather(j, slot):
            return pltpu.make_async_copy(table_hbm.at[idx_v.at[j]], rows_v.at[slot], sems.at[slot])

        gather(0, 0).start()

        @pl.loop(0, n_chunks, step=2)
        def _(j0):
            for slot in range(2):
                j = j0 + slot
                gather(j, slot).wait()

                @pl.when(j + 1 < n_chunks)
                def _():
                    gather(j + 1, 1 - slot).start()

                pltpu.sync_copy(rows_v.at[slot], out_hbm.at[pl.ds(base + j * GATHER_CHUNK, GATHER_CHUNK)])

    return pl.kernel(
        body,
        out_type=jax.ShapeDtypeStruct((n, d), table.dtype),
        mesh=mesh,
        scratch_types=[
            pltpu.VMEM((n_chunks, GATHER_CHUNK), jnp.int32),
            pltpu.VMEM((2, GATHER_CHUNK, d), table.dtype),
            pltpu.SemaphoreType.DMA((2,)),
        ],
        name="gather_rows",
    )(table, idx.reshape(workers, n_chunks, GATHER_CHUNK))


def _scatter_rows(table, pos_slots, n_out):
    t, d = table.shape
    n_slots = pos_slots.shape[0]
    workers = SC_CORES * SC_SUBCORES
    n_chunks = t // (workers * GATHER_CHUNK)
    assert t == workers * n_chunks * GATHER_CHUNK and n_chunks % 2 == 0
    per_worker = n_chunks * GATHER_CHUNK
    mesh = plsc.VectorSubcoreMesh(core_axis_name="c", subcore_axis_name="s")

    def body(table_hbm, pos_hbm, out_hbm, pos_v, rows_v, sems):
        wid = lax.axis_index("s") * SC_CORES + lax.axis_index("c")
        for k in range(n_slots):
            pltpu.sync_copy(pos_hbm.at[k, wid], pos_v.at[k])
        base = wid * per_worker

        def read(j, slot):
            return pltpu.make_async_copy(table_hbm.at[pl.ds(base + j * GATHER_CHUNK, GATHER_CHUNK)],
                                         rows_v.at[slot], sems.at[slot])

        read(0, 0).start()

        @pl.loop(0, n_chunks, step=2)
        def _(j0):
            for slot in range(2):
                j = j0 + slot
                read(j, slot).wait()

                @pl.when(j + 1 < n_chunks)
                def _():
                    read(j + 1, 1 - slot).start()

                for k in range(n_slots):
                    pltpu.sync_copy(rows_v.at[slot], out_hbm.at[pos_v.at[k, j]])

    return pl.kernel(
        body,
        out_type=jax.ShapeDtypeStruct((n_out, d), table.dtype),
        mesh=mesh,
        scratch_types=[
            pltpu.VMEM((n_slots, n_chunks, GATHER_CHUNK), jnp.int32),
            pltpu.VMEM((2, GATHER_CHUNK, d), table.dtype),
            pltpu.SemaphoreType.DMA((2,)),
        ],
        name="scatter_rows",
    )(table, pos_slots.reshape(n_slots, workers, n_chunks, GATHER_CHUNK))


def _expert_kernel(te_ref, tv_ref, nu_ref, x_ref, wg_ref, wu_ref, wd_ref, y_ref, x16_ref, acc_ref):
    i, f = pl.program_id(0), pl.program_id(1)
    live = i < nu_ref[0]

    @pl.when(f == 0)
    def _():
        acc_ref[...] = jnp.zeros_like(acc_ref)
        row = lax.broadcasted_iota(jnp.int32, (x_ref.shape[0], 1), 0)
        x = jnp.where(row < tv_ref[i], _unpack_bf16_pairs(x_ref[...]), 0.0)
        x16_ref[...] = x.astype(BF16)

    @pl.when(live)
    def _():
        h = x16_ref[...]
        gate = jnp.dot(h, wg_ref[0].astype(BF16), preferred_element_type=F32)
        up = jnp.dot(h, wu_ref[0].astype(BF16), preferred_element_type=F32)
        acc_ref[...] += jnp.dot((_silu(gate) * up).astype(BF16), wd_ref[0].astype(BF16),
                                preferred_element_type=F32)

    @pl.when(f == pl.num_programs(1) - 1)
    def _():
        y_ref[...] = _pack_bf16_pairs(acc_ref[...])


def _expert_ffn(xs, tile_expert, tile_valid, n_used, wg, wu, wd, *, tm, tf):
    n_rows = xs.shape[0]
    d_ff = wg.shape[2]
    n_f = d_ff // tf
    f_blk = lambda i, f, nu: jnp.where(i < nu[0], f, n_f - 1)
    grid_spec = pltpu.PrefetchScalarGridSpec(
        num_scalar_prefetch=3,
        grid=(n_rows // tm, n_f),
        in_specs=[
            pl.BlockSpec((tm, D_MODEL // 2), lambda i, f, te, tv, nu: (i, 0)),
            pl.BlockSpec((1, D_MODEL, tf), lambda i, f, te, tv, nu: (te[i], 0, f_blk(i, f, nu))),
            pl.BlockSpec((1, D_MODEL, tf), lambda i, f, te, tv, nu: (te[i], 0, f_blk(i, f, nu))),
            pl.BlockSpec((1, tf, D_MODEL), lambda i, f, te, tv, nu: (te[i], f_blk(i, f, nu), 0)),
        ],
        out_specs=pl.BlockSpec((tm, D_MODEL // 2), lambda i, f, te, tv, nu: (i, 0)),
        scratch_shapes=[pltpu.VMEM((tm, D_MODEL), BF16), pltpu.VMEM((tm, D_MODEL), F32)],
    )
    return pl.pallas_call(
        _expert_kernel,
        grid_spec=grid_spec,
        out_shape=jax.ShapeDtypeStruct((n_rows, D_MODEL // 2), jnp.uint32),
        compiler_params=_cparams("arbitrary", "arbitrary"),
        name="expert_ffn",
    )(tile_expert, tile_valid, n_used, xs, wg, wu, wd)


def _combine_kernel(y0_ref, y1_ref, gt_ref, x_ref, mod_ref, modc_ref, g_ref, b_ref, o_ref, *, tm, row0):
    is_ctx = _row_is_ctx(tm, pl.program_id(1), row0)
    gt = gt_ref[0]
    ffn = gt[:, 0:1] * _unpack_bf16_pairs(y0_ref[0, 0]) + gt[:, 1:2] * _unpack_bf16_pairs(y1_ref[0, 0])
    g2 = _mod_pick(is_ctx, modc_ref, mod_ref, 5)
    o_ref[0] = _layernorm_rows(DEEPNORM_ALPHA * x_ref[0] + (1.0 + g2) * ffn, g_ref[...], b_ref[...])


def _moe_combine(yg, gates, x1, mod, ln_g, ln_b, *, tm, row0):
    b, rows, _ = x1.shape
    row_spec = lambda n: pl.BlockSpec((1, tm, n), lambda i, j: (i, j, 0))
    vec = lambda n: pl.BlockSpec((1, n), lambda i, j: (0, 0))
    return pl.pallas_call(
        functools.partial(_combine_kernel, tm=tm, row0=row0),
        grid=(b, rows // tm),
        in_specs=[
            pl.BlockSpec((1, 1, tm, D_MODEL // 2), lambda i, j: (0, i, j, 0)),
            pl.BlockSpec((1, 1, tm, D_MODEL // 2), lambda i, j: (1, i, j, 0)),
            row_spec(LANES), row_spec(D_MODEL),
            pl.BlockSpec((1, 1, 6 * D_MODEL), lambda i, j: (i, 0, 0)),
            pl.BlockSpec((1, 1, 6 * D_MODEL), lambda i, j: (MOD_ROWS - 1, 0, 0)),
            vec(D_MODEL), vec(D_MODEL),
        ],
        out_specs=row_spec(D_MODEL),
        out_shape=jax.ShapeDtypeStruct((b, rows, D_MODEL), F32),
        compiler_params=_cparams("arbitrary", "arbitrary"),
        name="moe_combine",
    )(yg, yg, gates, x1, mod, mod, ln_g.reshape(1, -1), ln_b.reshape(1, -1))


def _routing_plan(eidx, *, tm):
    t = eidx.shape[0]
    n_tiles = (t * TOP_K) // tm + N_EXPERTS
    flat = eidx.reshape(-1)
    onehot = (flat[:, None] == jnp.arange(N_EXPERTS, dtype=jnp.int32)[None, :]).astype(jnp.int32)
    csum = jnp.cumsum(onehot, axis=0)
    counts = csum[-1]
    rank = jnp.sum((csum - onehot) * onehot, axis=1)
    tiles = (counts + tm - 1) // tm
    tile_end = jnp.cumsum(tiles)
    tile_start = tile_end - tiles
    pos = jnp.sum(onehot * tile_start[None, :], axis=1) * tm + rank
    n_used = tile_end[-1]
    tile_ids = jnp.arange(n_tiles, dtype=jnp.int32)
    tile_expert = jnp.sum((jnp.minimum(tile_ids, n_used - 1)[:, None] >= tile_end[None, :]).astype(jnp.int32), axis=1)
    rows_left = counts[tile_expert] - (tile_ids - tile_start[tile_expert]) * tm
    tile_valid = jnp.where(tile_ids < n_used, jnp.clip(rows_left, 0, tm), 0)
    return (pos.astype(jnp.int32), n_tiles * tm, tile_expert.astype(jnp.int32),
            tile_valid.astype(jnp.int32), n_used.reshape(1).astype(jnp.int32))


def _moe_ffn(h2, eidx, gates, x1, mod, wg, wu, wd, ln_g, ln_b, *, first_expert, row0, tm_row, tm_expert, tf):
    b, rows, _ = x1.shape
    t = b * rows
    pos, n_sorted, tile_expert, tile_valid, n_used = _routing_plan(eidx.reshape(t, LANES)[:, 0:TOP_K], tm=tm_expert)
    pos_slots = pos.reshape(t, TOP_K).T
    xs = _scatter_rows(h2.reshape(t, D_MODEL // 2), pos_slots, n_sorted)
    y = _expert_ffn(xs, tile_expert + first_expert, tile_valid, n_used, wg, wu, wd, tm=tm_expert, tf=tf)
    yg = _gather_rows(y, pos_slots.reshape(-1)).reshape(TOP_K, b, rows, D_MODEL // 2)
    return _moe_combine(yg, gates, x1, mod, ln_g, ln_b, tm=tm_row, row0=row0)


TM_ROW = 768
TM_LAST = 256
TF_DENSE = 1408
TM_EXPERT = 1024
TF_EXPERT = 512
SC_CORES = 2
SC_SUBCORES = 16
GATHER_CHUNK = 64


def kernel(x, c, ctx, c_ctx, w_mod, b_mod, w_in, w_out, na_rpb, conv_w, conv_b, dt_bias, a_log, d_skip,
           ssd_norm_w, ln_g, ln_b, ffn_w_gate, ffn_w_up, ffn_w_down, router_w, moe_w_gate, moe_w_up,
           moe_w_down):
    b, seq, _ = x.shape
    rows_lat = seq // GRID_W
    xa = jnp.concatenate([ctx, x], axis=1)
    cvec = jnp.zeros((MOD_ROWS, D_MODEL), F32).at[0:b].set(c).at[MOD_ROWS - 1].set(c_ctx)
    mod_all = _modulation(cvec, w_mod, b_mod)
    win_base, win_cfg_id, win_cfgs = _attn_window_plan(rows_lat)
    cos_t, sin_t = _rope_tables(seq)
    sel = _head_select_matrix()
    w_in_p = jnp.pad(w_in, ((0, 0), (0, 0), (0, D_IN_PAD - D_IN_PROJ))).astype(BF16)
    w_out16 = w_out.astype(BF16)

    for layer in range(DEPTH):
        last = layer == DEPTH - 1
        is_moe = layer % 2 == 1
        mod = mod_all[layer].reshape(MOD_ROWS, 1, 6 * D_MODEL)
        qkv, z, xbc, dt = _in_proj(xa, mod, w_in_p[layer], tm=TM_ROW)
        a = _attention(qkv, _rel_bias_rows(na_rpb[layer], win_cfgs), jnp.asarray(win_base),
                       jnp.asarray(win_cfg_id), n_cfg=len(win_cfgs), with_ctx_out=not last)
        y = _ssd_scan(xbc, dt, conv_w[layer], conv_b[layer], dt_bias[layer], a_log[layer], d_skip[layer],
                      cos_t, sin_t, sel)
        router_p = None
        if is_moe:
            r_pad = jnp.pad(router_w[layer // 2], ((0, 0), (0, LANES - N_EXPERTS)))
            r_hi = r_pad.astype(BF16)
            r_mid = (r_pad - r_hi.astype(F32)).astype(BF16)
            router_p = jnp.concatenate([r_hi, r_mid, r_hi], axis=0)
        row0, tm_post = (CTX_LEN, TM_LAST) if last else (0, TM_ROW)
        outs = _out_proj(a, y, z, xa, mod, w_out16[layer], ssd_norm_w[layer], ln_g[layer, 0], ln_b[layer, 0],
                         router_p, tm=tm_post, row0=row0)
        if is_moe:
            x1, h2, eidx, gates = outs
            m = layer // 2
            xa = _moe_ffn(h2, eidx, gates, x1, mod, moe_w_gate.reshape(-1, *moe_w_gate.shape[2:]),
                          moe_w_up.reshape(-1, *moe_w_up.shape[2:]), moe_w_down.reshape(-1, *moe_w_down.shape[2:]),
                          ln_g[layer, 1], ln_b[layer, 1], first_expert=m * N_EXPERTS, row0=row0,
                          tm_row=tm_post, tm_expert=TM_EXPERT, tf=TF_EXPERT)
        else:
            assert not last
            x1, h2 = outs
            m = layer // 2
            xa = _dense_ffn(h2, x1, mod, ffn_w_gate[m].astype(BF16), ffn_w_up[m].astype(BF16),
                            ffn_w_down[m].astype(BF16), ln_g[layer, 1], ln_b[layer, 1], tm=TM_ROW, tf=TF_DENSE)
    return xa
```

```python
import functools
import math

import numpy as np
import jax
import jax.numpy as jnp
from jax import lax
from jax.experimental import pallas as pl
from jax.experimental.pallas import tpu as pltpu
from jax.experimental.pallas import tpu_sc as plsc

F32 = jnp.float32
BF16 = jnp.bfloat16
HIGHEST = lax.Precision.HIGHEST

D_MODEL = 1024
DEPTH = 4
GRID_W = 64
CTX_LEN = 256
N_HEADS_NA = 8
HEAD_DIM_NA = 64
D_NA = N_HEADS_NA * HEAD_DIM_NA
WIN_R = 8
WIN_C = 16
N_HEADS_SSD = 8
HEAD_DIM_SSD = 64
D_SSD = N_HEADS_SSD * HEAD_DIM_SSD
N_GROUPS_SSD = 2
D_STATE = 128
D_CONV = 5
CHUNK = 128
ROPE_BASE = 10000.0
GN = N_GROUPS_SSD * D_STATE
CONV_CH = D_SSD + 2 * GN
D_IN_PROJ = 3 * D_NA + D_SSD + CONV_CH + 2 * N_HEADS_SSD
N_EXPERTS = 8
TOP_K = 2
DEEPNORM_ALPHA = (2.0 * DEPTH) ** 0.25
LN_EPS = 1e-5
RMS_EPS = 1e-5

LANES = 128
SUBLANES = 8
VMEM_LIMIT = 56 * 1024 * 1024
DT_PAD = LANES
D_IN_PAD = D_IN_PROJ - 2 * N_HEADS_SSD + DT_PAD
OFF_Z = 3 * D_NA
OFF_XBC = OFF_Z + D_SSD
OFF_DT = OFF_XBC + CONV_CH
MOD_ROWS = 24
HEAD_PAIR = 2 * HEAD_DIM_NA
ROWS_PER_STEP = 2
WIN_ROWS = WIN_R + ROWS_PER_STEP - 1


def _cparams(*sem):
    return pltpu.CompilerParams(dimension_semantics=sem, vmem_limit_bytes=VMEM_LIMIT)


def _silu(v):
    return v * jax.nn.sigmoid(v)


def _layernorm_rows(v, g, b):
    mu = jnp.mean(v, axis=-1, keepdims=True)
    d = v - mu
    var = jnp.mean(d * d, axis=-1, keepdims=True)
    return d * lax.rsqrt(var + LN_EPS) * g + b


def _pack_bf16_pairs(v):
    n = v.shape[1] // 2
    lo = pltpu.bitcast(v[:, 0:n].astype(BF16).astype(F32), jnp.uint32)
    hi = pltpu.bitcast(v[:, n:2 * n].astype(BF16).astype(F32), jnp.uint32)
    return hi | (lo >> 16)


def _unpack_bf16_pairs(w):
    lo = pltpu.bitcast(w << 16, F32)
    hi = pltpu.bitcast(w & jnp.uint32(0xFFFF0000), F32)
    return jnp.concatenate([lo, hi], axis=1)


def _row_is_ctx(tm, j, row0=0):
    row = lax.broadcasted_iota(jnp.int32, (tm, 1), 0) + j * tm + row0
    return row < CTX_LEN


def _mod_pick(is_ctx, modc_ref, mod_ref, seg):
    lo, hi = seg * D_MODEL, (seg + 1) * D_MODEL
    return jnp.where(is_ctx, modc_ref[0, :, lo:hi], mod_ref[0, :, lo:hi])


def _mod_kernel(c_ref, w_ref, b_ref, o_ref):
    cs = _silu(c_ref[...])
    o_ref[0] = jnp.dot(cs, w_ref[0], preferred_element_type=F32, precision=HIGHEST) + b_ref[0]


def _modulation(cvec, w_mod, b_mod):
    tn = D_MODEL
    return pl.pallas_call(
        _mod_kernel,
        grid=(DEPTH, 6 * D_MODEL // tn),
        in_specs=[
            pl.BlockSpec((MOD_ROWS, D_MODEL), lambda l, n: (0, 0)),
            pl.BlockSpec((1, D_MODEL, tn), lambda l, n: (l, 0, n)),
            pl.BlockSpec((1, 1, tn), lambda l, n: (l, 0, n)),
        ],
        out_specs=pl.BlockSpec((1, MOD_ROWS, tn), lambda l, n: (l, 0, n)),
        out_shape=jax.ShapeDtypeStruct((DEPTH, MOD_ROWS, 6 * D_MODEL), F32),
        compiler_params=_cparams("arbitrary", "arbitrary"),
        name="modulation",
    )(cvec, w_mod, b_mod.reshape(DEPTH, 1, 6 * D_MODEL))


def _in_proj_kernel(x_ref, mod_ref, modc_ref, w_ref, qkv_ref, z_ref, xbc_ref, dt_ref, *, tm):
    is_ctx = _row_is_ctx(tm, pl.program_id(1))
    sh = _mod_pick(is_ctx, modc_ref, mod_ref, 0)
    sc = _mod_pick(is_ctx, modc_ref, mod_ref, 1)
    h = (x_ref[0] * (1.0 + sc) + sh).astype(BF16)
    qkv_ref[0] = jnp.dot(h, w_ref[:, 0:OFF_Z], preferred_element_type=F32).astype(BF16)
    z_ref[0] = jnp.dot(h, w_ref[:, OFF_Z:OFF_XBC], preferred_element_type=F32)
    xbc_ref[0] = jnp.dot(h, w_ref[:, OFF_XBC:OFF_DT], preferred_element_type=F32)
    dt_ref[0] = jnp.dot(h, w_ref[:, OFF_DT:D_IN_PAD], preferred_element_type=F32)


def _in_proj(xa, mod, w_in_p, *, tm):
    b, rows, _ = xa.shape
    row_spec = lambda n: pl.BlockSpec((1, tm, n), lambda i, j: (i, j, 0))
    return pl.pallas_call(
        functools.partial(_in_proj_kernel, tm=tm),
        grid=(b, rows // tm),
        in_specs=[
            row_spec(D_MODEL),
            pl.BlockSpec((1, 1, 6 * D_MODEL), lambda i, j: (i, 0, 0)),
            pl.BlockSpec((1, 1, 6 * D_MODEL), lambda i, j: (MOD_ROWS - 1, 0, 0)),
            pl.BlockSpec((D_MODEL, D_IN_PAD), lambda i, j: (0, 0)),
        ],
        out_specs=[row_spec(OFF_Z), row_spec(D_SSD), row_spec(CONV_CH), row_spec(DT_PAD)],
        out_shape=[
            jax.ShapeDtypeStruct((b, rows, OFF_Z), BF16),
            jax.ShapeDtypeStruct((b, rows, D_SSD), F32),
            jax.ShapeDtypeStruct((b, rows, CONV_CH), F32),
            jax.ShapeDtypeStruct((b, rows, DT_PAD), F32),
        ],
        compiler_params=_cparams("arbitrary", "arbitrary"),
        name="in_proj",
    )(xa, mod, mod, w_in_p)


def _attn_head_pair(q_pair, keys_vals, biases):
    probs, denom = _attn_probs(q_pair, [k for k, _ in keys_vals], biases)
    return _attn_apply(probs, denom, [v for _, v in keys_vals])


def _attn_probs(q_pair, keys, biases):
    lane = lax.broadcasted_iota(jnp.int32, q_pair.shape, 1)
    zero = jnp.zeros_like(q_pair)
    q_stack = jnp.concatenate([jnp.where(lane < HEAD_DIM_NA, q_pair, zero),
                               jnp.where(lane >= HEAD_DIM_NA, q_pair, zero)], axis=0)
    scores = []
    for k, bias in zip(keys, biases):
        s = lax.dot_general(q_stack, k, (((1,), (1,)), ((), ())), preferred_element_type=F32)
        scores.append(s if bias is None else s + bias)
    m = scores[0].max(axis=-1, keepdims=True)
    for s in scores[1:]:
        m = jnp.maximum(m, s.max(axis=-1, keepdims=True))
    denom = jnp.zeros_like(m)
    probs = []
    for s in scores:
        p = jnp.exp(s - m)
        denom = denom + p.sum(axis=-1, keepdims=True)
        probs.append(p.astype(BF16))
    return probs, denom


def _attn_apply(probs, denom, values):
    m_rows = denom.shape[0] // 2
    acc = jnp.zeros((2 * m_rows, HEAD_PAIR), F32)
    for p, v in zip(probs, values):
        acc = acc + jnp.dot(p, v, preferred_element_type=F32)
    out = acc / denom
    lane = lax.broadcasted_iota(jnp.int32, (m_rows, HEAD_PAIR), 1)
    return jnp.where(lane < HEAD_DIM_NA, out[0:m_rows], out[m_rows:2 * m_rows])


def _attn_kernel(base_ref, cfg_ref, qkv_ref, rpb_ref, o_ref, bias_ref, *, rows_lat, n_cfg, with_ctx_out):
    scale = HEAD_DIM_NA ** -0.5
    k_off, v_off = D_NA, 2 * D_NA
    n_pairs = N_HEADS_NA // 2
    q_rows = ROWS_PER_STEP * GRID_W
    win = WIN_ROWS * GRID_W

    @pl.when(pl.program_id(0) == 0)
    def _():
        qc = lax.broadcasted_iota(jnp.int32, (GRID_W, GRID_W), 0)
        kc = lax.broadcasted_iota(jnp.int32, (GRID_W, GRID_W), 1)
        col_start = jnp.clip(qc - WIN_C // 2, 0, GRID_W - WIN_C)
        in_win = (kc >= col_start) & (kc < col_start + WIN_C)

        def expand(idx, carry):
            cfg, p = idx // n_pairs, idx % n_pairs
            for hh in range(2):
                for s in range(ROWS_PER_STEP):
                    for w in range(WIN_ROWS):
                        r = (((cfg * N_HEADS_NA + 2 * p + hh) * ROWS_PER_STEP + s) * WIN_ROWS + w)
                        v = jnp.broadcast_to(rpb_ref[pl.ds(r, 1), :], (GRID_W, LANES))
                        blk = pltpu.roll(v, LANES - (WIN_C - 1), 1, stride=1, stride_axis=0)[:, 0:GRID_W]
                        q0 = (hh * ROWS_PER_STEP + s) * GRID_W
                        bias_ref[cfg, p, q0:q0 + GRID_W, w * GRID_W:(w + 1) * GRID_W] = jnp.where(
                            in_win, blk, -jnp.inf)
            return carry

        lax.fori_loop(0, n_cfg * n_pairs, expand, 0)

    def lat_rows(j, carry):
        q_base = pl.multiple_of(CTX_LEN + j * q_rows, q_rows)
        k_base = pl.multiple_of(CTX_LEN + base_ref[j] * GRID_W, GRID_W)
        cfg = cfg_ref[j]

        def probs_of(p):
            lo = p * HEAD_PAIR
            q = qkv_ref[0, pl.ds(q_base, q_rows), lo:lo + HEAD_PAIR] * scale
            k_loc = qkv_ref[0, pl.ds(k_base, win), k_off + lo:k_off + lo + HEAD_PAIR]
            k_ctx = qkv_ref[0, 0:CTX_LEN, k_off + lo:k_off + lo + HEAD_PAIR]
            return _attn_probs(q, [k_loc, k_ctx], [bias_ref[cfg, p], None])

        def finish(p, probs, denom):
            lo = p * HEAD_PAIR
            v_loc = qkv_ref[0, pl.ds(k_base, win), v_off + lo:v_off + lo + HEAD_PAIR]
            v_ctx = qkv_ref[0, 0:CTX_LEN, v_off + lo:v_off + lo + HEAD_PAIR]
            o = _attn_apply(probs, denom, [v_loc, v_ctx])
            o_ref[0, pl.ds(q_base, q_rows), lo:lo + HEAD_PAIR] = o.astype(o_ref.dtype)

        pending = probs_of(0)
        for p in range(n_pairs):
            nxt = probs_of(p + 1) if p + 1 < n_pairs else None
            finish(p, *pending)
            pending = nxt
        return carry

    lax.fori_loop(0, rows_lat // ROWS_PER_STEP, lat_rows, 0)

    if with_ctx_out:
        for p in range(n_pairs):
            lo = p * HEAD_PAIR
            q = qkv_ref[0, 0:CTX_LEN, lo:lo + HEAD_PAIR] * scale
            k_ctx = qkv_ref[0, 0:CTX_LEN, k_off + lo:k_off + lo + HEAD_PAIR]
            v_ctx = qkv_ref[0, 0:CTX_LEN, v_off + lo:v_off + lo + HEAD_PAIR]
            o = _attn_head_pair(q, [(k_ctx, v_ctx)], [None])
            o_ref[0, 0:CTX_LEN, lo:lo + HEAD_PAIR] = o.astype(o_ref.dtype)
    else:
        o_ref[0, 0:CTX_LEN, :] = jnp.zeros((CTX_LEN, D_NA), o_ref.dtype)


def _attention(qkv, rpb_rows, win_base, cfg, *, n_cfg, with_ctx_out):
    b, rows, _ = qkv.shape
    rows_lat = (rows - CTX_LEN) // GRID_W
    grid_spec = pltpu.PrefetchScalarGridSpec(
        num_scalar_prefetch=2,
        grid=(b,),
        in_specs=[
            pl.BlockSpec((1, rows, OFF_Z), lambda i, wb, cf: (i, 0, 0)),
            pl.BlockSpec(rpb_rows.shape, lambda i, wb, cf: (0, 0)),
        ],
        out_specs=pl.BlockSpec((1, rows, D_NA), lambda i, wb, cf: (i, 0, 0)),
        scratch_shapes=[pltpu.VMEM((n_cfg, N_HEADS_NA // 2, 2 * ROWS_PER_STEP * GRID_W, WIN_ROWS * GRID_W), F32)],
    )
    return pl.pallas_call(
        functools.partial(_attn_kernel, rows_lat=rows_lat, n_cfg=n_cfg, with_ctx_out=with_ctx_out),
        grid_spec=grid_spec,
        out_shape=jax.ShapeDtypeStruct((b, rows, D_NA), BF16),
        compiler_params=_cparams("arbitrary"),
        name="attention",
    )(win_base, cfg, qkv, rpb_rows)


def _attn_window_plan(rows_lat):
    kr = min(WIN_R, rows_lat)
    assert rows_lat % ROWS_PER_STEP == 0 and rows_lat >= WIN_ROWS and kr == WIN_R
    bases, cfg_ids, configs = [], [], []
    for i in range(0, rows_lat, ROWS_PER_STEP):
        r0 = [int(np.clip(i + s - kr // 2, 0, rows_lat - kr)) for s in range(ROWS_PER_STEP)]
        base = min(r0[0], rows_lat - WIN_ROWS)
        assert all(0 <= r - base and r - base + kr <= WIN_ROWS for r in r0)
        c = tuple((i + s - r0[s], r0[s] - base) for s in range(ROWS_PER_STEP))
        if c not in configs:
            configs.append(c)
        bases.append(base)
        cfg_ids.append(configs.index(c))
    return np.asarray(bases, np.int32), np.asarray(cfg_ids, np.int32), configs


def _rel_bias_rows(rpb, configs):
    d = np.asarray([[c[s][0] for s in range(ROWS_PER_STEP)] for c in configs])[:, :, None]
    o = np.asarray([[c[s][1] for s in range(ROWS_PER_STEP)] for c in configs])[:, :, None]
    r = np.arange(WIN_ROWS)[None, None, :] - o
    row_ok = (r >= 0) & (r < WIN_R)
    row_idx = np.clip(r - d + WIN_R - 1, 0, 2 * WIN_R - 2)
    t = rpb.astype(F32)[:, row_idx]
    t = jnp.pad(t, ((0, 0),) * 4 + ((0, LANES - (2 * WIN_C - 1)),))
    t = jnp.where(jnp.asarray(row_ok)[None, :, :, :, None], t, -jnp.inf)
    return t.transpose(1, 0, 2, 3, 4).reshape(-1, LANES)


def _split3(v):
    hi = v.astype(BF16)
    rest = v - hi.astype(F32)
    mid = rest.astype(BF16)
    lo = (rest - mid.astype(F32)).astype(BF16)
    return hi, mid, lo


def _ssd_kernel(xbc_ref, dt_ref, convw_ref, convb_ref, dtb_ref, alog_ref, dskip_ref, cos_ref, sin_ref,
                sel_ref, y_ref, htf_ref, htb_ref, stf_ref, c_ref, ef_ref, ext_ref, *, n_chunks):
    n_ctx_chunks = CTX_LEN // CHUNK
    hp = D_SSD // N_GROUPS_SSD
    htf_ref[...] = jnp.zeros_like(htf_ref)
    htb_ref[...] = jnp.zeros_like(htb_ref)

    li = lax.broadcasted_iota(jnp.int32, (CHUNK, CHUNK), 0)
    si = lax.broadcasted_iota(jnp.int32, (CHUNK, CHUNK), 1)
    lower = li >= si
    upper = li <= si
    tri = jnp.concatenate([lower, upper], axis=0).astype(F32).astype(BF16)
    lane_c = lax.broadcasted_iota(jnp.int32, (CHUNK, LANES), 1)
    rope_lo = (lane_c % (D_STATE // 2)) < (D_STATE // 4)
    pair_lane = lax.broadcasted_iota(jnp.int32, (CHUNK, HEAD_PAIR), 1)

    def rope(w, cos, sin):
        partner = jnp.where(rope_lo, pltpu.roll(w, LANES - D_STATE // 4, 1), pltpu.roll(w, D_STATE // 4, 1))
        return w * cos + partner * sin

    def local_pass(k, carry):
        c = jnp.where(k < n_ctx_chunks, n_ctx_chunks - 1 - k, n_chunks + n_ctx_chunks - 1 - k)
        base = pl.multiple_of(c * CHUNK, CHUNK)
        is_start = (c == 0) | (c == n_ctx_chunks)
        is_end = (c == n_ctx_chunks - 1) | (c == n_chunks - 1)
        prev_base = pl.multiple_of(jnp.maximum(base - SUBLANES, 0), SUBLANES)
        next_base = pl.multiple_of(jnp.minimum(base + CHUNK, (n_chunks - 1) * CHUNK + CHUNK - SUBLANES), SUBLANES)
        ext_ref[0:SUBLANES, :] = jnp.where(is_start, 0.0, xbc_ref[0, pl.ds(prev_base, SUBLANES), :])
        ext_ref[SUBLANES:SUBLANES + CHUNK, :] = xbc_ref[0, pl.ds(base, CHUNK), :]
        ext_ref[SUBLANES + CHUNK:2 * SUBLANES + CHUNK, :] = jnp.where(
            is_end, 0.0, xbc_ref[0, pl.ds(next_base, SUBLANES), :])
        ext = ext_ref[...]
        conv = convb_ref[...]
        for t in range(D_CONV):
            shift = (D_CONV // 2 - t) % ext.shape[0]
            tap = ext if shift == 0 else pltpu.roll(ext, shift, 0)
            conv = conv + tap[SUBLANES:SUBLANES + CHUNK] * convw_ref[t:t + 1, :]
        u = _silu(conv)
        xs = u[:, 0:D_SSD]
        cos = cos_ref[pl.ds(base, CHUNK), :]
        sin = sin_ref[pl.ds(base, CHUNK), :]
        bm = [rope(u[:, D_SSD + g * D_STATE:D_SSD + (g + 1) * D_STATE], cos, sin) for g in range(N_GROUPS_SSD)]
        cm = [rope(u[:, D_SSD + GN + g * D_STATE:D_SSD + GN + (g + 1) * D_STATE], cos, sin)
              for g in range(N_GROUPS_SSD)]

        dt = jax.nn.softplus(dt_ref[0, pl.ds(base, CHUNK), :] + dtb_ref[...])
        a = dt * (-jnp.exp(alog_ref[...]))
        sums = jnp.dot(tri, jnp.concatenate(_split3(a), axis=1), preferred_element_type=F32)
        sums = sums[:, 0:LANES] + sums[:, LANES:2 * LANES] + sums[:, 2 * LANES:3 * LANES]
        acs = jnp.where(lane_c < N_HEADS_SSD, sums[0:CHUNK], sums[CHUNK:2 * CHUNK])
        acs_t = acs.T
        dt_t = dt.T
        wide = jnp.dot(jnp.concatenate(_split3(jnp.concatenate([acs, dt], axis=0)), axis=1), sel_ref[...],
                       preferred_element_type=F32)
        acs_x, dt_x = wide[0:CHUNK], wide[CHUNK:2 * CHUNK]
        arg_f, arg_b = acs_x[:, 0:D_SSD], acs_x[:, D_SSD:2 * D_SSD]
        e_f = jnp.exp(arg_f)
        e_b = jnp.exp(arg_b)
        x_f = (xs * dt_x[:, 0:D_SSD] * jnp.exp(arg_f[CHUNK - 1:CHUNK, :] - arg_f)).astype(BF16)
        x_b = (xs * dt_x[:, D_SSD:2 * D_SSD] * jnp.exp(arg_b[0:1, :] - arg_b)).astype(BF16)

        xs16 = xs.astype(BF16)
        y_parts = []
        for g in range(N_GROUPS_SSD):
            bt = bm[g].T.astype(BF16)
            c16 = cm[g].astype(BF16)
            gmat = jnp.dot(c16, bt, preferred_element_type=F32)
            gs = slice(g * hp, (g + 1) * hp)
            heads = range(g * (N_HEADS_SSD // N_GROUPS_SSD), (g + 1) * (N_HEADS_SSD // N_GROUPS_SSD))
            ms = []
            for h in heads:
                jf, jb = h, N_HEADS_SSD + h
                wf = jnp.where(lower, jnp.exp(acs[:, jf:jf + 1] - acs_t[jf:jf + 1, :]), 0.0) * dt_t[jf:jf + 1, :]
                wb = jnp.where(upper, jnp.exp(acs[:, jb:jb + 1] - acs_t[jb:jb + 1, :]), 0.0) * dt_t[jb:jb + 1, :]
                ms.append((gmat * (wf + wb)).astype(BF16))
            y_g = []
            for q in range(len(ms) // 2):
                lo = g * hp + q * HEAD_PAIR
                xp = xs16[:, lo:lo + HEAD_PAIR]
                rhs = jnp.concatenate([jnp.where(pair_lane < HEAD_DIM_SSD, xp, jnp.zeros_like(xp)),
                                       jnp.where(pair_lane >= HEAD_DIM_SSD, xp, jnp.zeros_like(xp))], axis=0)
                lhs = jnp.concatenate([ms[2 * q], ms[2 * q + 1]], axis=1)
                y_g.append(jnp.dot(lhs, rhs, preferred_element_type=F32))
            y_diag = jnp.concatenate(y_g, axis=1)
            y_off = jnp.dot(c16, htb_ref[g].astype(BF16), preferred_element_type=F32) * e_b[:, gs]
            y_parts.append(y_diag + y_off)
            htb_ref[g] = htb_ref[g] * e_b[0:1, gs] + jnp.dot(bt, x_b[:, gs], preferred_element_type=F32)
            stf_ref[c, g] = jnp.dot(bt, x_f[:, gs], preferred_element_type=F32)
            c_ref[c, g] = c16
        ef_ref[c] = e_f
        y_ref[0, pl.ds(base, CHUNK), :] = jnp.concatenate(y_parts, axis=1) + dskip_ref[...] * xs
        return carry

    lax.fori_loop(0, n_chunks, local_pass, 0)

    def forward_pass(c, carry):
        base = pl.multiple_of(c * CHUNK, CHUNK)
        e_f = ef_ref[c]
        y_parts = []
        for g in range(N_GROUPS_SSD):
            gs = slice(g * hp, (g + 1) * hp)
            y_parts.append(jnp.dot(c_ref[c, g], htf_ref[g].astype(BF16), preferred_element_type=F32) * e_f[:, gs])
            htf_ref[g] = htf_ref[g] * e_f[CHUNK - 1:CHUNK, gs] + stf_ref[c, g]
        y_ref[0, pl.ds(base, CHUNK), :] = y_ref[0, pl.ds(base, CHUNK), :] + jnp.concatenate(y_parts, axis=1)
        return carry

    lax.fori_loop(0, n_chunks, forward_pass, 0)


def _ssd_scan(xbc, dt, conv_w, conv_b, dt_bias, a_log, d_skip, cos_t, sin_t, sel):
    b, rows, _ = xbc.shape
    n_chunks = rows // CHUNK
    hp = D_SSD // N_GROUPS_SSD
    full = lambda a: pl.BlockSpec(a.shape, lambda i: (0,) * a.ndim)
    pad16 = lambda v: jnp.pad(v.reshape(1, -1), ((0, 0), (0, DT_PAD - v.size)))
    conv_w8 = jnp.pad(conv_w, ((0, SUBLANES - D_CONV), (0, 0)))
    params = [conv_w8, conv_b.reshape(1, -1), pad16(dt_bias), pad16(a_log),
              jnp.repeat(d_skip, HEAD_DIM_SSD).reshape(1, -1), cos_t, sin_t, sel]
    return pl.pallas_call(
        functools.partial(_ssd_kernel, n_chunks=n_chunks),
        grid=(b,),
        in_specs=[pl.BlockSpec((1, rows, CONV_CH), lambda i: (i, 0, 0)),
                  pl.BlockSpec((1, rows, DT_PAD), lambda i: (i, 0, 0))] + [full(p) for p in params],
        out_specs=pl.BlockSpec((1, rows, D_SSD), lambda i: (i, 0, 0)),
        out_shape=jax.ShapeDtypeStruct((b, rows, D_SSD), F32),
        scratch_shapes=[
            pltpu.VMEM((N_GROUPS_SSD, D_STATE, hp), F32),
            pltpu.VMEM((N_GROUPS_SSD, D_STATE, hp), F32),
            pltpu.VMEM((n_chunks, N_GROUPS_SSD, D_STATE, hp), F32),
            pltpu.VMEM((n_chunks, N_GROUPS_SSD, CHUNK, D_STATE), BF16),
            pltpu.VMEM((n_chunks, CHUNK, D_SSD), F32),
            pltpu.VMEM((CHUNK + 2 * SUBLANES, CONV_CH), F32),
        ],
        compiler_params=_cparams("arbitrary"),
        name="ssd_scan",
    )(xbc, dt, *params)


def _rope_tables(seq):
    half, quarter = D_STATE // 2, D_STATE // 4
    t = jnp.arange(seq)
    inv_freq = ROPE_BASE ** (-jnp.arange(quarter, dtype=F32) / quarter)
    ang_r = (t // GRID_W).astype(F32)[:, None] * inv_freq
    ang_c = (t % GRID_W).astype(F32)[:, None] * inv_freq
    cos = jnp.concatenate([jnp.cos(ang_r)] * 2 + [jnp.cos(ang_c)] * 2, axis=-1)
    sin = jnp.concatenate([-jnp.sin(ang_r), jnp.sin(ang_r), -jnp.sin(ang_c), jnp.sin(ang_c)], axis=-1)
    cos = jnp.concatenate([jnp.ones((CTX_LEN, 2 * half), F32), cos], axis=0)
    sin = jnp.concatenate([jnp.zeros((CTX_LEN, 2 * half), F32), sin], axis=0)
    return cos, sin


def _head_select_matrix():
    sel = np.zeros((DT_PAD, 2 * D_SSD), np.float32)
    for j in range(2 * N_HEADS_SSD):
        sel[j, j * HEAD_DIM_SSD:(j + 1) * HEAD_DIM_SSD] = 1.0
    return jnp.asarray(np.concatenate([sel] * 3, axis=0), dtype=BF16)


def _out_proj_kernel(*refs, tm, with_router, row0):
    if with_router:
        (a_ref, y_ref, z_ref, x_ref, mod_ref, modc_ref, w_ref, nw_ref, g_ref, b_ref, r_ref,
         x1_ref, h2_ref, ei_ref, gt_ref) = refs
    else:
        a_ref, y_ref, z_ref, x_ref, mod_ref, modc_ref, w_ref, nw_ref, g_ref, b_ref, x1_ref, h2_ref = refs
    is_ctx = _row_is_ctx(tm, pl.program_id(1), row0)
    yg = y_ref[0] * _silu(z_ref[0])
    gw = D_SSD // N_GROUPS_SSD
    parts = []
    for g in range(N_GROUPS_SSD):
        part = yg[:, g * gw:(g + 1) * gw]
        parts.append(part * lax.rsqrt(jnp.mean(part * part, axis=-1, keepdims=True) + RMS_EPS))
    s = (jnp.concatenate(parts, axis=1) * nw_ref[...]).astype(BF16)
    mix = (jnp.dot(a_ref[0], w_ref[0:D_NA, :], preferred_element_type=F32)
           + jnp.dot(s, w_ref[D_NA:D_NA + D_SSD, :], preferred_element_type=F32))
    g1 = _mod_pick(is_ctx, modc_ref, mod_ref, 2)
    x1 = _layernorm_rows(DEEPNORM_ALPHA * x_ref[0] + (1.0 + g1) * mix, g_ref[...], b_ref[...])
    x1_ref[0] = x1
    sh2 = _mod_pick(is_ctx, modc_ref, mod_ref, 3)
    sc2 = _mod_pick(is_ctx, modc_ref, mod_ref, 4)
    h2 = x1 * (1.0 + sc2) + sh2
    if with_router:
        h2_ref[0] = _pack_bf16_pairs(h2)
    else:
        h2_ref[0] = h2.astype(h2_ref.dtype)
    if with_router:
        h_hi = h2.astype(BF16)
        h_mid = (h2 - h_hi.astype(F32)).astype(BF16)
        logits = jnp.dot(jnp.concatenate([h_hi, h_hi, h_mid], axis=1), r_ref[...],
                         preferred_element_type=F32)
        lane = lax.broadcasted_iota(jnp.int32, logits.shape, 1)
        logits = jnp.where(lane < N_EXPERTS, logits, -jnp.inf)
        m1 = logits.max(axis=-1, keepdims=True)
        i1 = jnp.where(logits == m1, lane, LANES).min(axis=-1, keepdims=True)
        rest = jnp.where(lane == i1, -jnp.inf, logits)
        m2 = rest.max(axis=-1, keepdims=True)
        i2 = jnp.where(rest == m2, lane, LANES).min(axis=-1, keepdims=True)
        e2 = jnp.exp(m2 - m1)
        gate1 = 1.0 / (1.0 + e2)
        gate2 = e2 / (1.0 + e2)
        ei_ref[0] = jnp.where(lane == 0, i1, jnp.where(lane == 1, i2, 0))
        gt_ref[0] = jnp.where(lane == 0, gate1, jnp.where(lane == 1, gate2, 0.0))


def _out_proj(a, y, z, xa, mod, w_out, norm_w, ln_g, ln_b, router_p, *, tm, row0=0):
    b, rows_in, _ = xa.shape
    rows = rows_in - row0
    assert row0 % tm == 0 and rows % tm == 0
    with_router = router_p is not None
    row_spec = lambda n: pl.BlockSpec((1, tm, n), lambda i, j: (i, j, 0))
    in_row_spec = lambda n: pl.BlockSpec((1, tm, n), lambda i, j: (i, j + row0 // tm, 0))
    vec = lambda n: pl.BlockSpec((1, n), lambda i, j: (0, 0))
    in_specs = [
        in_row_spec(D_NA), in_row_spec(D_SSD), in_row_spec(D_SSD), in_row_spec(D_MODEL),
        pl.BlockSpec((1, 1, 6 * D_MODEL), lambda i, j: (i, 0, 0)),
        pl.BlockSpec((1, 1, 6 * D_MODEL), lambda i, j: (MOD_ROWS - 1, 0, 0)),
        pl.BlockSpec((D_NA + D_SSD, D_MODEL), lambda i, j: (0, 0)),
        vec(D_SSD), vec(D_MODEL), vec(D_MODEL),
    ]
    args = [a, y, z, xa, mod, mod, w_out, norm_w.reshape(1, -1), ln_g.reshape(1, -1), ln_b.reshape(1, -1)]
    h2_cols, h2_dtype = (D_MODEL // 2, jnp.uint32) if with_router else (D_MODEL, BF16)
    out_specs = [row_spec(D_MODEL), row_spec(h2_cols)]
    out_shape = [jax.ShapeDtypeStruct((b, rows, D_MODEL), F32), jax.ShapeDtypeStruct((b, rows, h2_cols), h2_dtype)]
    if with_router:
        in_specs.append(pl.BlockSpec((3 * D_MODEL, LANES), lambda i, j: (0, 0)))
        args.append(router_p)
        out_specs += [row_spec(LANES), row_spec(LANES)]
        out_shape += [jax.ShapeDtypeStruct((b, rows, LANES), jnp.int32), jax.ShapeDtypeStruct((b, rows, LANES), F32)]
    return pl.pallas_call(
        functools.partial(_out_proj_kernel, tm=tm, with_router=with_router, row0=row0),
        grid=(b, rows // tm),
        in_specs=in_specs,
        out_specs=out_specs,
        out_shape=out_shape,
        compiler_params=_cparams("arbitrary", "arbitrary"),
        name="out_proj",
    )(*args)


def _ffn_kernel(h_ref, x_ref, mod_ref, modc_ref, wg_ref, wu_ref, wd_ref, g_ref, b_ref, o_ref, *, tm, tf):
    is_ctx = _row_is_ctx(tm, pl.program_id(1))
    h = h_ref[0]
    d_ff = wg_ref.shape[1]
    acc = jnp.zeros((tm, D_MODEL), F32)
    for f in range(d_ff // tf):
        fs = slice(f * tf, (f + 1) * tf)
        gate = jnp.dot(h, wg_ref[:, fs], preferred_element_type=F32)
        up = jnp.dot(h, wu_ref[:, fs], preferred_element_type=F32)
        acc = acc + jnp.dot((_silu(gate) * up).astype(BF16), wd_ref[fs, :], preferred_element_type=F32)
    g2 = _mod_pick(is_ctx, modc_ref, mod_ref, 5)
    o_ref[0] = _layernorm_rows(DEEPNORM_ALPHA * x_ref[0] + (1.0 + g2) * acc, g_ref[...], b_ref[...])


def _dense_ffn(h2, x1, mod, wg, wu, wd, ln_g, ln_b, *, tm, tf):
    b, rows, _ = x1.shape
    d_ff = wg.shape[1]
    row_spec = lambda n: pl.BlockSpec((1, tm, n), lambda i, j: (i, j, 0))
    vec = lambda n: pl.BlockSpec((1, n), lambda i, j: (0, 0))
    const = lambda s: pl.BlockSpec(s, lambda i, j: (0, 0), pipeline_mode=pl.Buffered(1))
    return pl.pallas_call(
        functools.partial(_ffn_kernel, tm=tm, tf=tf),
        grid=(b, rows // tm),
        in_specs=[
            row_spec(D_MODEL), row_spec(D_MODEL),
            pl.BlockSpec((1, 1, 6 * D_MODEL), lambda i, j: (i, 0, 0)),
            pl.BlockSpec((1, 1, 6 * D_MODEL), lambda i, j: (MOD_ROWS - 1, 0, 0)),
            const((D_MODEL, d_ff)), const((D_MODEL, d_ff)), const((d_ff, D_MODEL)),
            vec(D_MODEL), vec(D_MODEL),
        ],
        out_specs=row_spec(D_MODEL),
        out_shape=jax.ShapeDtypeStruct((b, rows, D_MODEL), F32),
        compiler_params=_cparams("arbitrary", "arbitrary"),
        name="dense_ffn",
    )(h2, x1, mod, mod, wg, wu, wd, ln_g.reshape(1, -1), ln_b.reshape(1, -1))


def _gather_rows(table, idx):
    n, d = idx.shape[0], table.shape[1]
    workers = SC_CORES * SC_SUBCORES
    n_chunks = n // (workers * GATHER_CHUNK)
    assert n == workers * n_chunks * GATHER_CHUNK
    per_worker = n_chunks * GATHER_CHUNK
    mesh = plsc.VectorSubcoreMesh(core_axis_name="c", subcore_axis_name="s")

    assert n_chunks % 2 == 0

    def body(table_hbm, idx_hbm, out_hbm, idx_v, rows_v, sems):
        wid = lax.axis_index("s") * SC_CORES + lax.axis_index("c")
        pltpu.sync_copy(idx_hbm.at[wid], idx_v)
        base = wid * per_worker

        def gather(j, slot):
            return pltpu.make_async_copy(table_hbm.at[idx_v.at[j]], rows_v.at[slot], sems.at[slot])

        gather(0, 0).start()

        @pl.loop(0, n_chunks, step=2)
        def _(j0):
            for slot in range(2):
                j = j0 + slot
                gather(j, slot).wait()

                @pl.when(j + 1 < n_chunks)
                def _():
                    gather(j + 1, 1 - slot).start()

                pltpu.sync_copy(rows_v.at[slot], out_hbm.at[pl.ds(base + j * GATHER_CHUNK, GATHER_CHUNK)])

    return pl.kernel(
        body,
        out_type=jax.ShapeDtypeStruct((n, d), table.dtype),
        mesh=mesh,
        scratch_types=[
            pltpu.VMEM((n_chunks, GATHER_CHUNK), jnp.int32),
            pltpu.VMEM((2, GATHER_CHUNK, d), table.dtype),
            pltpu.SemaphoreType.DMA((2,)),
        ],
        name="gather_rows",
    )(table, idx.reshape(workers, n_chunks, GATHER_CHUNK))


def _scatter_rows(table, pos_slots, n_out):
    t, d = table.shape
    n_slots = pos_slots.shape[0]
    workers = SC_CORES * SC_SUBCORES
    n_chunks = t // (workers * GATHER_CHUNK)
    assert t == workers * n_chunks * GATHER_CHUNK and n_chunks % 2 == 0
    per_worker = n_chunks * GATHER_CHUNK
    mesh = plsc.VectorSubcoreMesh(core_axis_name="c", subcore_axis_name="s")

    def body(table_hbm, pos_hbm, out_hbm, pos_v, rows_v, sems):
        wid = lax.axis_index("s") * SC_CORES + lax.axis_index("c")
        for k in range(n_slots):
            pltpu.sync_copy(pos_hbm.at[k, wid], pos_v.at[k])
        base = wid * per_worker

        def read(j, slot):
            return pltpu.make_async_copy(table_hbm.at[pl.ds(base + j * GATHER_CHUNK, GATHER_CHUNK)],
                                         rows_v.at[slot], sems.at[slot])

        read(0, 0).start()

        @pl.loop(0, n_chunks, step=2)
        def _(j0):
            for slot in range(2):
                j = j0 + slot
                read(j, slot).wait()

                @pl.when(j + 1 < n_chunks)
                def _():
                    read(j + 1, 1 - slot).start()

                for k in range(n_slots):
                    pltpu.sync_copy(rows_v.at[slot], out_hbm.at[pos_v.at[k, j]])

    return pl.kernel(
        body,
        out_type=jax.ShapeDtypeStruct((n_out, d), table.dtype),
        mesh=mesh,
        scratch_types=[
            pltpu.VMEM((n_slots, n_chunks, GATHER_CHUNK), jnp.int32),
            pltpu.VMEM((2, GATHER_CHUNK, d), table.dtype),
            pltpu.SemaphoreType.DMA((2,)),
        ],
        name="scatter_rows",
    )(table, pos_slots.reshape(n_slots, workers, n_chunks, GATHER_CHUNK))


def _expert_kernel(te_ref, tv_ref, nu_ref, x_ref, wg_ref, wu_ref, wd_ref, y_ref, x16_ref, acc_ref):
    i, f = pl.program_id(0), pl.program_id(1)
    live = i < nu_ref[0]

    @pl.when(f == 0)
    def _():
        acc_ref[...] = jnp.zeros_like(acc_ref)
        row = lax.broadcasted_iota(jnp.int32, (x_ref.shape[0], 1), 0)
        x = jnp.where(row < tv_ref[i], _unpack_bf16_pairs(x_ref[...]), 0.0)
        x16_ref[...] = x.astype(BF16)

    @pl.when(live)
    def _():
        h = x16_ref[...]
        gate = jnp.dot(h, wg_ref[0].astype(BF16), preferred_element_type=F32)
        up = jnp.dot(h, wu_ref[0].astype(BF16), preferred_element_type=F32)
        acc_ref[...] += jnp.dot((_silu(gate) * up).astype(BF16), wd_ref[0].astype(BF16),
                                preferred_element_type=F32)

    @pl.when(f == pl.num_programs(1) - 1)
    def _():
        y_ref[...] = _pack_bf16_pairs(acc_ref[...])


def _expert_ffn(xs, tile_expert, tile_valid, n_used, wg, wu, wd, *, tm, tf):
    n_rows = xs.shape[0]
    d_ff = wg.shape[2]
    n_f = d_ff // tf
    f_blk = lambda i, f, nu: jnp.where(i < nu[0], f, n_f - 1)
    grid_spec = pltpu.PrefetchScalarGridSpec(
        num_scalar_prefetch=3,
        grid=(n_rows // tm, n_f),
        in_specs=[
            pl.BlockSpec((tm, D_MODEL // 2), lambda i, f, te, tv, nu: (i, 0)),
            pl.BlockSpec((1, D_MODEL, tf), lambda i, f, te, tv, nu: (te[i], 0, f_blk(i, f, nu))),
            pl.BlockSpec((1, D_MODEL, tf), lambda i, f, te, tv, nu: (te[i], 0, f_blk(i, f, nu))),
            pl.BlockSpec((1, tf, D_MODEL), lambda i, f, te, tv, nu: (te[i], f_blk(i, f, nu), 0)),
        ],
        out_specs=pl.BlockSpec((tm, D_MODEL // 2), lambda i, f, te, tv, nu: (i, 0)),
        scratch_shapes=[pltpu.VMEM((tm, D_MODEL), BF16), pltpu.VMEM((tm, D_MODEL), F32)],
    )
    return pl.pallas_call(
        _expert_kernel,
        grid_spec=grid_spec,
        out_shape=jax.ShapeDtypeStruct((n_rows, D_MODEL // 2), jnp.uint32),
        compiler_params=_cparams("arbitrary", "arbitrary"),
        name="expert_ffn",
    )(tile_expert, tile_valid, n_used, xs, wg, wu, wd)


def _combine_kernel(y0_ref, y1_ref, gt_ref, x_ref, mod_ref, modc_ref, g_ref, b_ref, o_ref, *, tm, row0):
    is_ctx = _row_is_ctx(tm, pl.program_id(1), row0)
    gt = gt_ref[0]
    ffn = gt[:, 0:1] * _unpack_bf16_pairs(y0_ref[0, 0]) + gt[:, 1:2] * _unpack_bf16_pairs(y1_ref[0, 0])
    g2 = _mod_pick(is_ctx, modc_ref, mod_ref, 5)
    o_ref[0] = _layernorm_rows(DEEPNORM_ALPHA * x_ref[0] + (1.0 + g2) * ffn, g_ref[...], b_ref[...])


def _moe_combine(yg, gates, x1, mod, ln_g, ln_b, *, tm, row0):
    b, rows, _ = x1.shape
    row_spec = lambda n: pl.BlockSpec((1, tm, n), lambda i, j: (i, j, 0))
    vec = lambda n: pl.BlockSpec((1, n), lambda i, j: (0, 0))
    return pl.pallas_call(
        functools.partial(_combine_kernel, tm=tm, row0=row0),
        grid=(b, rows // tm),
        in_specs=[
            pl.BlockSpec((1, 1, tm, D_MODEL // 2), lambda i, j: (0, i, j, 0)),
            pl.BlockSpec((1, 1, tm, D_MODEL // 2), lambda i, j: (1, i, j, 0)),
            row_spec(LANES), row_spec(D_MODEL),
            pl.BlockSpec((1, 1, 6 * D_MODEL), lambda i, j: (i, 0, 0)),
            pl.BlockSpec((1, 1, 6 * D_MODEL), lambda i, j: (MOD_ROWS - 1, 0, 0)),
            vec(D_MODEL), vec(D_MODEL),
        ],
        out_specs=row_spec(D_MODEL),
        out_shape=jax.ShapeDtypeStruct((b, rows, D_MODEL), F32),
        compiler_params=_cparams("arbitrary", "arbitrary"),
        name="moe_combine",
    )(yg, yg, gates, x1, mod, mod, ln_g.reshape(1, -1), ln_b.reshape(1, -1))


def _routing_plan(eidx, *, tm):
    t = eidx.shape[0]
    n_tiles = (t * TOP_K) // tm + N_EXPERTS
    flat = eidx.reshape(-1)
    onehot = (flat[:, None] == jnp.arange(N_EXPERTS, dtype=jnp.int32)[None, :]).astype(jnp.int32)
    csum = jnp.cumsum(onehot, axis=0)
    counts = csum[-1]
    rank = jnp.sum((csum - onehot) * onehot, axis=1)
    tiles = (counts + tm - 1) // tm
    tile_end = jnp.cumsum(tiles)
    tile_start = tile_end - tiles
    pos = jnp.sum(onehot * tile_start[None, :], axis=1) * tm + rank
    n_used = tile_end[-1]
    tile_ids = jnp.arange(n_tiles, dtype=jnp.int32)
    tile_expert = jnp.sum((jnp.minimum(tile_ids, n_used - 1)[:, None] >= tile_end[None, :]).astype(jnp.int32), axis=1)
    rows_left = counts[tile_expert] - (tile_ids - tile_start[tile_expert]) * tm
    tile_valid = jnp.where(tile_ids < n_used, jnp.clip(rows_left, 0, tm), 0)
    return (pos.astype(jnp.int32), n_tiles * tm, tile_expert.astype(jnp.int32),
            tile_valid.astype(jnp.int32), n_used.reshape(1).astype(jnp.int32))


def _moe_ffn(h2, eidx, gates, x1, mod, wg, wu, wd, ln_g, ln_b, *, first_expert, row0, tm_row, tm_expert, tf):
    b, rows, _ = x1.shape
    t = b * rows
    pos, n_sorted, tile_expert, tile_valid, n_used = _routing_plan(eidx.reshape(t, LANES)[:, 0:TOP_K], tm=tm_expert)
    pos_slots = pos.reshape(t, TOP_K).T
    xs = _scatter_rows(h2.reshape(t, D_MODEL // 2), pos_slots, n_sorted)
    y = _expert_ffn(xs, tile_expert + first_expert, tile_valid, n_used, wg, wu, wd, tm=tm_expert, tf=tf)
    yg = _gather_rows(y, pos_slots.reshape(-1)).reshape(TOP_K, b, rows, D_MODEL // 2)
    return _moe_combine(yg, gates, x1, mod, ln_g, ln_b, tm=tm_row, row0=row0)


TM_ROW = 768
TM_LAST = 256
TF_DENSE = 1408
TM_EXPERT = 1024
TF_EXPERT = 512
SC_CORES = 2
SC_SUBCORES = 16
GATHER_CHUNK = 64


def kernel(x, c, ctx, c_ctx, w_mod, b_mod, w_in, w_out, na_rpb, conv_w, conv_b, dt_bias, a_log, d_skip,
           ssd_norm_w, ln_g, ln_b, ffn_w_gate, ffn_w_up, ffn_w_down, router_w, moe_w_gate, moe_w_up,
           moe_w_down):
    b, seq, _ = x.shape
    rows_lat = seq // GRID_W
    xa = jnp.concatenate([ctx, x], axis=1)
    cvec = jnp.zeros((MOD_ROWS, D_MODEL), F32).at[0:b].set(c).at[MOD_ROWS - 1].set(c_ctx)
    mod_all = _modulation(cvec, w_mod, b_mod)
    win_base, win_cfg_id, win_cfgs = _attn_window_plan(rows_lat)
    cos_t, sin_t = _rope_tables(seq)
    sel = _head_select_matrix()
    w_in_p = jnp.pad(w_in, ((0, 0), (0, 0), (0, D_IN_PAD - D_IN_PROJ))).astype(BF16)
    w_out16 = w_out.astype(BF16)

    for layer in range(DEPTH):
        last = layer == DEPTH - 1
        is_moe = layer % 2 == 1
        mod = mod_all[layer].reshape(MOD_ROWS, 1, 6 * D_MODEL)
        qkv, z, xbc, dt = _in_proj(xa, mod, w_in_p[layer], tm=TM_ROW)
        a = _attention(qkv, _rel_bias_rows(na_rpb[layer], win_cfgs), jnp.asarray(win_base),
                       jnp.asarray(win_cfg_id), n_cfg=len(win_cfgs), with_ctx_out=not last)
        y = _ssd_scan(xbc, dt, conv_w[layer], conv_b[layer], dt_bias[layer], a_log[layer], d_skip[layer],
                      cos_t, sin_t, sel)
        router_p = None
        if is_moe:
            r_pad = jnp.pad(router_w[layer // 2], ((0, 0), (0, LANES - N_EXPERTS)))
            r_hi = r_pad.astype(BF16)
            r_mid = (r_pad - r_hi.astype(F32)).astype(BF16)
            router_p = jnp.concatenate([r_hi, r_mid, r_hi], axis=0)
        row0, tm_post = (CTX_LEN, TM_LAST) if last else (0, TM_ROW)
        outs = _out_proj(a, y, z, xa, mod, w_out16[layer], ssd_norm_w[layer], ln_g[layer, 0], ln_b[layer, 0],
                         router_p, tm=tm_post, row0=row0)
        if is_moe:
            x1, h2, eidx, gates = outs
            m = layer // 2
            xa = _moe_ffn(h2, eidx, gates, x1, mod, moe_w_gate.reshape(-1, *moe_w_gate.shape[2:]),
                          moe_w_up.reshape(-1, *moe_w_up.shape[2:]), moe_w_down.reshape(-1, *moe_w_down.shape[2:]),
                          ln_g[layer, 1], ln_b[layer, 1], first_expert=m * N_EXPERTS, row0=row0,
                          tm_row=tm_post, tm_expert=TM_EXPERT, tf=TF_EXPERT)
        else:
            assert not last
            x1, h2 = outs
            m = layer // 2
            xa = _dense_ffn(h2, x1, mod, ffn_w_gate[m].astype(BF16), ffn_w_up[m].astype(BF16),
                            ffn_w_down[m].astype(BF16), ln_g[layer, 1], ln_b[layer, 1], tm=TM_ROW, tf=TF_DENSE)
    return xa
```
